```python
import math
import jax, jax.numpy as jnp
from jax import lax
import numpy as np

D_MODEL = 2048
BATCH = 2
SEQ = 8192
DEPTH = 1

HEAD_DIM = 128
BLOCK = 128
EPS = 1e-6
DIL_PATTERNS = ((128, 1), (512, 4), (2048, 16))
N_GROUPS = len(DIL_PATTERNS)
A_HEADS_PER_GROUP = 8
A_HEADS = N_GROUPS * A_HEADS_PER_GROUP
A_WIDTH = A_HEADS * HEAD_DIM
A_OUT = A_HEADS_PER_GROUP * HEAD_DIM
B_Q_HEADS = 16
B_KV_HEADS = 2
B_GROUP = B_Q_HEADS // B_KV_HEADS
B_WINDOW = 128
B_Q_WIDTH = B_Q_HEADS * HEAD_DIM
B_KV_WIDTH = B_KV_HEADS * HEAD_DIM
D_FF = 4 * D_MODEL
N_ATTN_HEADS = B_Q_HEADS + A_HEADS
OFF_QA = 0
OFF_KA = OFF_QA + A_WIDTH
OFF_VA = OFF_KA + A_WIDTH
OFF_QB = OFF_VA + A_WIDTH
OFF_KB = OFF_QB + B_Q_WIDTH
OFF_VB = OFF_KB + B_KV_WIDTH
OFF_GA = OFF_VB + B_KV_WIDTH
OFF_GB = OFF_GA + D_MODEL
IN_COLS = OFF_GB + D_MODEL

kernel_name = "hybrid_dilated_swa_sink_gated_block"


def alibi_slopes():
    i = np.arange(1, N_ATTN_HEADS + 1, dtype=np.float32)
    return (2.0 ** (-8.0 * i / N_ATTN_HEADS)).astype(np.float32)


def rmsnorm(x, g):
    xf = x.astype(jnp.float32)
    y = xf * lax.rsqrt(jnp.mean(xf * xf, axis=-1, keepdims=True) + EPS)
    return (y * g.astype(jnp.float32)).astype(x.dtype)


def banded_attention(q, k, v, slopes, stride, max_back, sink=None):
    N, L0, Hk, G, D = q.shape
    pad = (-L0) % BLOCK
    if pad:
        q = jnp.pad(q, ((0, 0), (0, pad), (0, 0), (0, 0), (0, 0)))
        k = jnp.pad(k, ((0, 0), (0, pad), (0, 0), (0, 0)))
        v = jnp.pad(v, ((0, 0), (0, pad), (0, 0), (0, 0)))
    L = L0 + pad
    nb = L // BLOCK
    qb = q.reshape(N, nb, BLOCK, Hk, G, D)

    def band(a):
        cur = a.reshape(N, nb, BLOCK, Hk, D)
        prev = jnp.pad(cur, ((0, 0), (1, 0), (0, 0), (0, 0), (0, 0)))[:, :-1]
        return jnp.concatenate([prev, cur], axis=2)

    kb, vb = band(k), band(v)
    s = jnp.einsum('nbqhgd,nbkhd->nbhgqk', qb, kb).astype(jnp.float32) * (HEAD_DIM ** -0.5)
    qi = np.arange(BLOCK) + BLOCK
    ki = np.arange(2 * BLOCK)
    rel = qi[:, None] - ki[None, :]
    kpos = np.arange(nb)[:, None, None] * BLOCK - BLOCK + ki[None, None, :]
    valid = ((rel >= 0) & (rel <= max_back))[None] & (kpos >= 0)
    bias = -slopes.astype(jnp.float32)[:, :, None, None] * jnp.asarray(stride * rel, jnp.float32)
    s = jnp.where(jnp.asarray(valid)[None, :, None, None], s + bias[None, None], -jnp.inf)
    m = jnp.max(s, axis=-1)
    if sink is not None:
        sink_f = sink.astype(jnp.float32)[None, None, :, :, None]
        m = jnp.maximum(m, sink_f)
    p = jnp.exp(s - m[..., None])
    denom = jnp.sum(p, axis=-1)
    if sink is not None:
        denom = denom + jnp.exp(sink_f - m)
    o = jnp.einsum('nbhgqk,nbkhd->nbqhgd', p.astype(v.dtype), vb).astype(jnp.float32)
    denom_q = jnp.moveaxis(denom, -1, 2)
    o = o / denom_q[..., None]
    lse = jnp.moveaxis(m, -1, 2) + jnp.log(denom_q)
    o = o.reshape(N, L, Hk, G, D)[:, :L0]
    lse = lse.reshape(N, L, Hk, G)[:, :L0]
    return o, lse


def dilated_group(q, k, v, slopes, window, dilation):
    Bn, T, H, D = q.shape
    Ls = T // dilation

    def gather(a):
        return a.reshape(Bn, Ls, dilation, H, D).transpose(0, 2, 1, 3, 4).reshape(Bn * dilation, Ls, H, D)

    o, lse = banded_attention(gather(q)[:, :, :, None, :], gather(k), gather(v),
                              slopes[:, None], dilation, window // dilation)
    o = o.reshape(Bn, dilation, Ls, H, D).transpose(0, 2, 1, 3, 4).reshape(Bn, T, H, D)
    lse = lse.reshape(Bn, dilation, Ls, H).transpose(0, 2, 1, 3).reshape(Bn, T, H)
    return o, lse


def setup_inputs(seed: int = 0) -> dict:
    key = jax.random.key(seed)
    ks = jax.random.split(key, 15)
    f32 = jnp.float32

    def w(k, shape, fan_in):
        return jax.random.normal(k, shape, f32) * (fan_in ** -0.5)

    def gain(k, shape):
        return 1.0 + 0.1 * jax.random.normal(k, shape, f32)

    return {
        "x": jax.random.normal(ks[0], (BATCH, SEQ, D_MODEL), f32),
        "norm1_g": gain(ks[1], (DEPTH, D_MODEL)),
        "w_in": w(ks[2], (DEPTH, D_MODEL, IN_COLS), D_MODEL),
        "q_norm_a": gain(ks[3], (DEPTH, HEAD_DIM)),
        "k_norm_a": gain(ks[4], (DEPTH, HEAD_DIM)),
        "q_norm_b": gain(ks[5], (DEPTH, HEAD_DIM)),
        "k_norm_b": gain(ks[6], (DEPTH, HEAD_DIM)),
        "sinks_b": 0.5 * jax.random.normal(ks[7], (DEPTH, B_Q_HEADS), f32),
        "w_branch_a": w(ks[8], (DEPTH, A_OUT, D_MODEL), A_OUT),
        "w_branch_b": w(ks[9], (DEPTH, B_Q_WIDTH, D_MODEL), B_Q_WIDTH),
        "w_out": w(ks[10], (DEPTH, D_MODEL, D_MODEL), D_MODEL),
        "norm2_g": gain(ks[11], (DEPTH, D_MODEL)),
        "w_ff1": w(ks[12], (DEPTH, D_MODEL, D_FF), D_MODEL),
        "w_ff2": w(ks[13], (DEPTH, D_FF, D_MODEL), D_FF),
    }


def reference(x, norm1_g, w_in, q_norm_a, k_norm_a, q_norm_b, k_norm_b, sinks_b,
              w_branch_a, w_branch_b, w_out, norm2_g, w_ff1, w_ff2):
    Bn, T, _ = x.shape
    slopes = jnp.asarray(alibi_slopes())
    slopes_b = slopes[:B_Q_HEADS].reshape(B_KV_HEADS, B_GROUP)
    slopes_a = slopes[B_Q_HEADS:].reshape(N_GROUPS, A_HEADS_PER_GROUP)
    for l in range(DEPTH):
        h = rmsnorm(x, norm1_g[l])
        proj = h @ w_in[l]
        shp_a = (Bn, T, N_GROUPS, A_HEADS_PER_GROUP, HEAD_DIM)
        qa = rmsnorm(proj[..., OFF_QA:OFF_KA].reshape(shp_a), q_norm_a[l])
        ka = rmsnorm(proj[..., OFF_KA:OFF_VA].reshape(shp_a), k_norm_a[l])
        va = proj[..., OFF_VA:OFF_QB].reshape(shp_a)
        outs, lses = [], []
        for g, (window, dilation) in enumerate(DIL_PATTERNS):
            o_g, lse_g = dilated_group(qa[:, :, g], ka[:, :, g], va[:, :, g], slopes_a[g], window, dilation)
            outs.append(o_g)
            lses.append(lse_g)
        alpha = jax.nn.softmax(jnp.stack(lses, axis=0), axis=0)
        o_a = jnp.sum(alpha[..., None] * jnp.stack(outs, axis=0), axis=0)
        o_a = o_a.reshape(Bn, T, A_OUT).astype(x.dtype)

        qb = rmsnorm(proj[..., OFF_QB:OFF_KB].reshape(Bn, T, B_KV_HEADS, B_GROUP, HEAD_DIM), q_norm_b[l])
        kb = rmsnorm(proj[..., OFF_KB:OFF_VB].reshape(Bn, T, B_KV_HEADS, HEAD_DIM), k_norm_b[l])
        vb = proj[..., OFF_VB:OFF_GA].reshape(Bn, T, B_KV_HEADS, HEAD_DIM)
        o_b, _ = banded_attention(qb, kb, vb, slopes_b, 1, B_WINDOW - 1,
                                  sinks_b[l].reshape(B_KV_HEADS, B_GROUP))
        o_b = o_b.reshape(Bn, T, B_Q_WIDTH).astype(x.dtype)

        gate_a = jax.nn.sigmoid(proj[..., OFF_GA:OFF_GB])
        gate_b = jax.nn.sigmoid(proj[..., OFF_GB:IN_COLS])
        merged = gate_a * (o_a @ w_branch_a[l]) + gate_b * (o_b @ w_branch_b[l])
        x = x + merged @ w_out[l]
        h2 = rmsnorm(x, norm2_g[l])
        x = x + jnp.square(jax.nn.relu(h2 @ w_ff1[l])) @ w_ff2[l]
    return x
```

```python
import functools

import numpy as np
import jax
import jax.numpy as jnp
from jax import lax
from jax.experimental import pallas as pl
from jax.experimental.pallas import tpu as pltpu

D_MODEL = 2048
HEAD_DIM = 128
BLOCK = 128
EPS = 1e-6
DILATIONS = (1, 4, 16)
A_MAX_BACK = 128
N_GROUPS = 3
A_HPG = 8
A_WIDTH = N_GROUPS * A_HPG * HEAD_DIM
B_Q_HEADS = 16
B_KV_HEADS = 2
B_GROUP = B_Q_HEADS // B_KV_HEADS
B_MAX_BACK = 127
D_FF = 4 * D_MODEL
N_ATTN_HEADS = B_Q_HEADS + N_GROUPS * A_HPG
IN_COLS = 3 * A_WIDTH + B_Q_HEADS * HEAD_DIM + 2 * B_KV_HEADS * HEAD_DIM + 2 * D_MODEL

SUPER = BLOCK * DILATIONS[-1]
NEG_INF = float("-inf")
BF16 = jnp.bfloat16
F32 = jnp.float32

VMEM_LIMIT = 56 * 1024 * 1024

TM_IN = 1024
TN_IN = 512
HEADS_PER_TILE = TN_IN // HEAD_DIM
A_TILES = 3 * A_WIDTH // TN_IN
TILES_PER_ROLE = A_WIDTH // TN_IN
TILES_PER_GROUP = A_HPG * HEAD_DIM // TN_IN
QB_TILE0 = A_TILES
QB_TILES = B_Q_HEADS * HEAD_DIM // TN_IN
KVB_TILE = QB_TILE0 + QB_TILES
GATE_TILE0 = KVB_TILE + 1
GATE_TILES = 2 * D_MODEL // TN_IN
N_TILES_IN = GATE_TILE0 + GATE_TILES

TQ_B = 1024
TM_MERGE = 512
TM_FFN = 512
TF_FFN = 512


def _alibi_slopes():
    i = np.arange(1, N_ATTN_HEADS + 1, dtype=np.float32)
    return (2.0 ** (-8.0 * i / N_ATTN_HEADS)).astype(np.float32)


def _rms_scale(a, gain):
    ms = jnp.mean(a * a, axis=-1, keepdims=True)
    return a * lax.rsqrt(ms + EPS) * gain


def _dot(a, b):
    return jnp.dot(a, b, preferred_element_type=F32)


def _dot_nt(a, b):
    return lax.dot_general(a, b, (((1,), (1,)), ((), ())), preferred_element_type=F32)


def _inproj_kernel(x_ref, g1_ref, w_ref, qkg_ref, a0_ref, a1_ref, a2_ref, qb_ref, kvb_ref,
                   gate_ref, h_scr, acc_scr):
    j = pl.program_id(1)
    tm = x_ref.shape[0]

    @pl.when(j == 0)
    def _():
        rows = 256
        for c in range(tm // rows):
            x = x_ref[c * rows:(c + 1) * rows, :]
            h_scr[c * rows:(c + 1) * rows, :] = _rms_scale(x, g1_ref[...]).astype(BF16)

    acc = _dot(h_scr[...], w_ref[...])
    for c in range(HEADS_PER_TILE):
        acc_scr[c] = acc[:, c * HEAD_DIM:(c + 1) * HEAD_DIM]

    role = j // TILES_PER_ROLE
    group = (j % TILES_PER_ROLE) // TILES_PER_GROUP
    is_a = j < A_TILES
    gain_a = jnp.where(role == 0, qkg_ref[0:1, :], qkg_ref[1:2, :])

    def write_a(out_ref, d, normed):
        for c in range(HEADS_PER_TILE):
            for res in range(d):
                if d == 1:
                    a = acc_scr[c]
                else:
                    a = acc_scr[c, pl.ds(res, tm // d, stride=d), :]
                if normed:
                    a = _rms_scale(a, gain_a)
                if d == 1:
                    out_ref[c] = a.astype(BF16)
                else:
                    out_ref[c, res] = a.astype(BF16)

    for g, (d, out_ref) in enumerate(zip(DILATIONS, (a0_ref, a1_ref, a2_ref))):
        @pl.when(is_a & (group == g) & (role < 2))
        def _(out_ref=out_ref, d=d):
            write_a(out_ref, d, True)

        @pl.when(is_a & (group == g) & (role == 2))
        def _(out_ref=out_ref, d=d):
            write_a(out_ref, d, False)

    @pl.when((j >= QB_TILE0) & (j < KVB_TILE))
    def _():
        for c in range(HEADS_PER_TILE):
            qb_ref[c] = _rms_scale(acc_scr[c], qkg_ref[2:3, :]).astype(BF16)

    @pl.when(j == KVB_TILE)
    def _():
        for c in range(B_KV_HEADS):
            kvb_ref[c] = _rms_scale(acc_scr[c], qkg_ref[3:4, :]).astype(BF16)
        for c in range(B_KV_HEADS, 2 * B_KV_HEADS):
            kvb_ref[c] = acc_scr[c].astype(BF16)

    @pl.when(j >= GATE_TILE0)
    def _():
        for c in range(HEADS_PER_TILE):
            z = acc_scr[c]
            gate_ref[:, c * HEAD_DIM:(c + 1) * HEAD_DIM] = (1.0 / (1.0 + jnp.exp(-z))).astype(BF16)


def _in_projection(x2, g1, w_in, qk_gains, bsz, seq):
    m = bsz * seq
    tm = TM_IN
    tiles_per_seq = seq // tm
    grid = (m // tm, N_TILES_IN)

    def role_of(j):
        return jnp.clip(j // TILES_PER_ROLE, 0, 2)

    def a_map(g):
        def hb(j):
            return jnp.clip(j - TILES_PER_ROLE * role_of(j) - TILES_PER_GROUP * g, 0, TILES_PER_GROUP - 1)
        if g == 0:
            return lambda i, j: (role_of(j), hb(j), i, 0)
        return lambda i, j: (role_of(j), hb(j), i // tiles_per_seq, 0, i % tiles_per_seq, 0)

    out_shapes = [
        jax.ShapeDtypeStruct((3, A_HPG, m, HEAD_DIM), BF16),
        jax.ShapeDtypeStruct((3, A_HPG, bsz, 4, seq // 4, HEAD_DIM), BF16),
        jax.ShapeDtypeStruct((3, A_HPG, bsz, 16, seq // 16, HEAD_DIM), BF16),
        jax.ShapeDtypeStruct((B_Q_HEADS, m, HEAD_DIM), BF16),
        jax.ShapeDtypeStruct((2 * B_KV_HEADS, m, HEAD_DIM), BF16),
        jax.ShapeDtypeStruct((m, 2 * D_MODEL), BF16),
    ]
    out_specs = [
        pl.BlockSpec((None, HEADS_PER_TILE, tm, HEAD_DIM), a_map(0)),
        pl.BlockSpec((None, HEADS_PER_TILE, None, 4, tm // 4, HEAD_DIM), a_map(1)),
        pl.BlockSpec((None, HEADS_PER_TILE, None, 16, tm // 16, HEAD_DIM), a_map(2)),
        pl.BlockSpec((HEADS_PER_TILE, tm, HEAD_DIM),
                     lambda i, j: (jnp.clip(j - QB_TILE0, 0, QB_TILES - 1), i, 0)),
        pl.BlockSpec((2 * B_KV_HEADS, tm, HEAD_DIM), lambda i, j: (0, i, 0)),
        pl.BlockSpec((tm, TN_IN), lambda i, j: (i, jnp.clip(j - GATE_TILE0, 0, GATE_TILES - 1))),
    ]
    in_specs = [
        pl.BlockSpec((tm, D_MODEL), lambda i, j: (i, 0)),
        pl.BlockSpec((1, D_MODEL), lambda i, j: (0, 0)),
        pl.BlockSpec((D_MODEL, TN_IN), lambda i, j: (0, j)),
        pl.BlockSpec((4, HEAD_DIM), lambda i, j: (0, 0)),
    ]
    return pl.pallas_call(
        _inproj_kernel,
        grid=grid,
        in_specs=in_specs,
        out_specs=out_specs,
        out_shape=out_shapes,
        scratch_shapes=[pltpu.VMEM((tm, D_MODEL), BF16),
                        pltpu.VMEM((HEADS_PER_TILE, tm, HEAD_DIM), F32)],
        compiler_params=pltpu.CompilerParams(
            dimension_semantics=("arbitrary", "arbitrary"), vmem_limit_bytes=VMEM_LIMIT),
        name="in_projection",
    )(x2, g1, w_in, qk_gains)


def _band_bias(slope_times_stride, max_back):
    qi = lax.broadcasted_iota(jnp.int32, (BLOCK, 2 * BLOCK), 0) + BLOCK
    ki = lax.broadcasted_iota(jnp.int32, (BLOCK, 2 * BLOCK), 1)
    rel = qi - ki
    valid = (rel >= 0) & (rel <= max_back)
    bias = jnp.where(valid, -slope_times_stride * rel.astype(F32), NEG_INF)
    return bias, ki


def _attn_unit(q, kw, vw, bias):
    s = _dot_nt(q, kw) + bias
    m = jnp.max(s, axis=-1, keepdims=True)
    p = jnp.exp(s - m)
    l = jnp.sum(p, axis=-1, keepdims=True)
    o = _dot(p.astype(BF16), vw) / l
    return o, m + jnp.log(l)


def _mixer_a_kernel(slopes_ref,
                    q0_ref, k0c_ref, k0p_ref, v0c_ref, v0p_ref,
                    q1_ref, k1c_ref, k1p_ref, v1c_ref, v1p_ref,
                    q2_ref, k2c_ref, k2p_ref, v2c_ref, v2p_ref,
                    o_ref, o1_scr, l1_scr, o2_scr, l2_scr):
    first = pl.program_id(1) == 0
    h = pl.program_id(2)

    biases = []
    for g, d in enumerate(DILATIONS):
        bias, ki = _band_bias(slopes_ref[g * A_HPG + h] * float(d), A_MAX_BACK)
        bias_head = jnp.where(first, jnp.where(ki < BLOCK, NEG_INF, bias), bias)
        biases.append((bias, bias_head))

    def window(cur_ref, prev_ref, r, blk):
        if blk == 0:
            cur = cur_ref[0:BLOCK] if r is None else cur_ref[r, 0:BLOCK]
            prev = prev_ref[...] if r is None else prev_ref[r]
            return jnp.concatenate([prev, cur], axis=0)
        lo = (blk - 1) * BLOCK
        return cur_ref[lo:lo + 2 * BLOCK] if r is None else cur_ref[r, lo:lo + 2 * BLOCK]

    def body2(r, carry):
        kw = jnp.concatenate([k2p_ref[r], k2c_ref[r]], axis=0)
        vw = jnp.concatenate([v2p_ref[r], v2c_ref[r]], axis=0)
        o, lse = _attn_unit(q2_ref[r], kw, vw, biases[2][1])
        o2_scr[pl.ds(r, BLOCK, stride=16), :] = o
        l2_scr[pl.ds(r, BLOCK, stride=16), :] = jnp.broadcast_to(lse, (BLOCK, HEAD_DIM))
        return carry

    lax.fori_loop(0, 16, body2, 0, unroll=2)

    def body1(r, carry):
        for blk in range(4):
            kw = window(k1c_ref, k1p_ref, r, blk)
            vw = window(v1c_ref, v1p_ref, r, blk)
            q = q1_ref[r, blk * BLOCK:(blk + 1) * BLOCK]
            o, lse = _attn_unit(q, kw, vw, biases[1][1] if blk == 0 else biases[1][0])
            rows = pl.ds(blk * BLOCK * 4 + r, BLOCK, stride=4)
            o1_scr[rows, :] = o
            l1_scr[rows, :] = jnp.broadcast_to(lse, (BLOCK, HEAD_DIM))
        return carry

    lax.fori_loop(0, 4, body1, 0)

    def finish(blk, kw, vw, bias):
        start = 0 if isinstance(blk, int) else pl.multiple_of(blk * BLOCK, BLOCK)
        rows = pl.ds(start, BLOCK)
        o0, lse0 = _attn_unit(q0_ref[rows, :], kw, vw, bias)
        lse0 = jnp.broadcast_to(lse0, (BLOCK, HEAD_DIM))
        lse1 = l1_scr[rows, :]
        lse2 = l2_scr[rows, :]
        mx = jnp.maximum(jnp.maximum(lse0, lse1), lse2)
        w0 = jnp.exp(lse0 - mx)
        w1 = jnp.exp(lse1 - mx)
        w2 = jnp.exp(lse2 - mx)
        tot = w0 + w1 + w2
        o = (w0 * o0 + w1 * o1_scr[rows, :] + w2 * o2_scr[rows, :]) / tot
        o_ref[rows, :] = o.astype(BF16)

    finish(0, window(k0c_ref, k0p_ref, None, 0), window(v0c_ref, v0p_ref, None, 0), biases[0][1])

    def body0(blk, carry):
        win = pl.ds(pl.multiple_of((blk - 1) * BLOCK, BLOCK), 2 * BLOCK)
        finish(blk, k0c_ref[win, :], v0c_ref[win, :], biases[0][0])
        return carry

    lax.fori_loop(1, SUPER // BLOCK, body0, 0, unroll=3)


def _mixer_a(a0, a1, a2, slopes_a, bsz, seq):
    m = bsz * seq
    nsb = seq // SUPER
    grid = (bsz, nsb, A_HPG)

    def nat(role):
        return pl.BlockSpec((None, None, SUPER, HEAD_DIM), lambda b, s, h: (role, h, b * nsb + s, 0))

    def nat_prev(role):
        per = SUPER // BLOCK
        return pl.BlockSpec((None, None, BLOCK, HEAD_DIM),
                            lambda b, s, h: (role, h, jnp.maximum((b * nsb + s) * per - 1, 0), 0))

    def dil(role, d):
        return pl.BlockSpec((None, None, None, d, SUPER // d, HEAD_DIM),
                            lambda b, s, h: (role, h, b, 0, s, 0))

    def dil_prev(role, d):
        per = SUPER // d // BLOCK
        return pl.BlockSpec((None, None, None, d, BLOCK, HEAD_DIM),
                            lambda b, s, h: (role, h, b, 0, jnp.maximum(s * per - 1, 0), 0))

    in_specs = [pl.BlockSpec(memory_space=pltpu.SMEM),
                nat(0), nat(1), nat_prev(1), nat(2), nat_prev(2),
                dil(0, 4), dil(1, 4), dil_prev(1, 4), dil(2, 4), dil_prev(2, 4),
                dil(0, 16), dil(1, 16), dil_prev(1, 16), dil(2, 16), dil_prev(2, 16)]
    return pl.pallas_call(
        _mixer_a_kernel,
        grid=grid,
        in_specs=in_specs,
        out_specs=pl.BlockSpec((None, SUPER, HEAD_DIM), lambda b, s, h: (h, b * nsb + s, 0)),
        out_shape=jax.ShapeDtypeStruct((A_HPG, m, HEAD_DIM), BF16),
        scratch_shapes=[pltpu.VMEM((SUPER, HEAD_DIM), F32) for _ in range(4)],
        compiler_params=pltpu.CompilerParams(
            dimension_semantics=("arbitrary", "arbitrary", "arbitrary"), vmem_limit_bytes=VMEM_LIMIT),
        name="mixer_a",
    )(slopes_a, a0, a0, a0, a0, a0, a1, a1, a1, a1, a1, a2, a2, a2, a2, a2)


def _mixer_b_kernel(slopes_ref, sinks_ref, q_ref, kc_ref, kp_ref, vc_ref, vp_ref, o_ref, *, tiles_per_seq):
    first = (pl.program_id(0) % tiles_per_seq) == 0
    kvh = pl.program_id(1)
    tq = kc_ref.shape[0]

    biases, biases_head = [], []
    for hh in range(B_GROUP):
        bias, ki = _band_bias(slopes_ref[kvh * B_GROUP + hh], B_MAX_BACK)
        biases.append(bias)
        biases_head.append(jnp.where(first, jnp.where(ki < BLOCK, NEG_INF, bias), bias))

    for blk in range(tq // BLOCK):
        if blk == 0:
            kw = jnp.concatenate([kp_ref[...], kc_ref[0:BLOCK]], axis=0)
            vw = jnp.concatenate([vp_ref[...], vc_ref[0:BLOCK]], axis=0)
        else:
            kw = kc_ref[(blk - 1) * BLOCK:(blk + 1) * BLOCK]
            vw = vc_ref[(blk - 1) * BLOCK:(blk + 1) * BLOCK]
        q = q_ref[:, blk * BLOCK:(blk + 1) * BLOCK, :].reshape(B_GROUP * BLOCK, HEAD_DIM)
        s_all = _dot_nt(q, kw)
        ps, denoms = [], []
        for hh in range(B_GROUP):
            s = s_all[hh * BLOCK:(hh + 1) * BLOCK] + (biases_head[hh] if blk == 0 else biases[hh])
            sink = sinks_ref[kvh * B_GROUP + hh]
            mrow = jnp.maximum(jnp.max(s, axis=-1, keepdims=True), sink)
            p = jnp.exp(s - mrow)
            denoms.append(jnp.sum(p, axis=-1, keepdims=True) + jnp.exp(sink - mrow))
            ps.append(p.astype(BF16))
        o_all = _dot(jnp.concatenate(ps, axis=0), vw)
        for hh in range(B_GROUP):
            o = o_all[hh * BLOCK:(hh + 1) * BLOCK] / denoms[hh]
            o_ref[hh, blk * BLOCK:(blk + 1) * BLOCK, :] = o.astype(BF16)


def _mixer_b(qb, kvb, slopes_b, sinks, bsz, seq):
    m = bsz * seq
    tq = TQ_B
    per = tq // BLOCK
    grid = (m // tq, B_KV_HEADS)

    def cur(off):
        return pl.BlockSpec((None, tq, HEAD_DIM), lambda i, c: (c + off, i, 0))

    def prev(off):
        return pl.BlockSpec((None, BLOCK, HEAD_DIM), lambda i, c: (c + off, jnp.maximum(i * per - 1, 0), 0))

    smem = pl.BlockSpec(memory_space=pltpu.SMEM)
    return pl.pallas_call(
        functools.partial(_mixer_b_kernel, tiles_per_seq=seq // tq),
        grid=grid,
        in_specs=[smem, smem,
                  pl.BlockSpec((B_GROUP, tq, HEAD_DIM), lambda i, c: (c, i, 0)),
                  cur(0), prev(0), cur(B_KV_HEADS), prev(B_KV_HEADS)],
        out_specs=pl.BlockSpec((B_GROUP, tq, HEAD_DIM), lambda i, c: (c, i, 0)),
        out_shape=jax.ShapeDtypeStruct((B_Q_HEADS, m, HEAD_DIM), BF16),
        compiler_params=pltpu.CompilerParams(
            dimension_semantics=("arbitrary", "arbitrary"), vmem_limit_bytes=VMEM_LIMIT),
        name="mixer_b",
    )(slopes_b, sinks, qb, kvb, kvb, kvb, kvb)


def _merge_kernel(oa_ref, ob_ref, gate_ref, wa_ref, wb_ref, out_ref):
    oa = jnp.concatenate([oa_ref[hh] for hh in range(A_HPG)], axis=-1)
    ob = jnp.concatenate([ob_ref[hh] for hh in range(B_Q_HEADS)], axis=-1)
    ga = gate_ref[:, 0:D_MODEL].astype(F32)
    gb = gate_ref[:, D_MODEL:2 * D_MODEL].astype(F32)
    merged = ga * _dot(oa, wa_ref[...]) + gb * _dot(ob, wb_ref[...])
    out_ref[...] = merged.astype(BF16)


def _merge(oa, ob, gates, wa, wb):
    m = gates.shape[0]
    tm = TM_MERGE
    resident = dict(pipeline_mode=pl.Buffered(1))
    return pl.pallas_call(
        _merge_kernel,
        grid=(m // tm,),
        in_specs=[pl.BlockSpec((A_HPG, tm, HEAD_DIM), lambda i: (0, i, 0)),
                  pl.BlockSpec((B_Q_HEADS, tm, HEAD_DIM), lambda i: (0, i, 0)),
                  pl.BlockSpec((tm, 2 * D_MODEL), lambda i: (i, 0)),
                  pl.BlockSpec(wa.shape, lambda i: (0, 0), **resident),
                  pl.BlockSpec(wb.shape, lambda i: (0, 0), **resident)],
        out_specs=pl.BlockSpec((tm, D_MODEL), lambda i: (i, 0)),
        out_shape=jax.ShapeDtypeStruct((m, D_MODEL), BF16),
        compiler_params=pltpu.CompilerParams(
            dimension_semantics=("arbitrary",), vmem_limit_bytes=VMEM_LIMIT),
        name="gated_merge",
    )(oa, ob, gates, wa, wb)


def _ffn_kernel(x_ref, mg_ref, wo_ref, g2_ref, w1_ref, w2_ref, out_ref, h2_scr):
    @pl.when(pl.program_id(1) == 0)
    def _():
        x1 = x_ref[...] + _dot(mg_ref[...], wo_ref[...])
        out_ref[...] = x1
        h2_scr[...] = _rms_scale(x1, g2_ref[...]).astype(BF16)

    z = _dot(h2_scr[...], w1_ref[...])
    act = jnp.square(jnp.maximum(z, 0.0)).astype(BF16)
    out_ref[...] += _dot(act, w2_ref[...])


def _ffn(x2, merged, wo, g2, w1, w2):
    m = x2.shape[0]
    tm, tf = TM_FFN, TF_FFN
    return pl.pallas_call(
        _ffn_kernel,
        grid=(m // tm, D_FF // tf),
        in_specs=[pl.BlockSpec((tm, D_MODEL), lambda i, f: (i, 0)),
                  pl.BlockSpec((tm, D_MODEL), lambda i, f: (i, 0)),
                  pl.BlockSpec(wo.shape, lambda i, f: (0, 0), pipeline_mode=pl.Buffered(1)),
                  pl.BlockSpec((1, D_MODEL), lambda i, f: (0, 0)),
                  pl.BlockSpec((D_MODEL, tf), lambda i, f: (0, f)),
                  pl.BlockSpec((tf, D_MODEL), lambda i, f: (f, 0))],
        out_specs=pl.BlockSpec((tm, D_MODEL), lambda i, f: (i, 0)),
        out_shape=jax.ShapeDtypeStruct((m, D_MODEL), F32),
        scratch_shapes=[pltpu.VMEM((tm, D_MODEL), BF16)],
        compiler_params=pltpu.CompilerParams(
            dimension_semantics=("arbitrary", "arbitrary"), vmem_limit_bytes=VMEM_LIMIT),
        name="out_proj_ffn",
    )(x2, merged, wo, g2, w1, w2)


def kernel(x, norm1_g, w_in, q_norm_a, k_norm_a, q_norm_b, k_norm_b, sinks_b, w_branch_a, w_branch_b,
           w_out, norm2_g, w_ff1, w_ff2):
    bsz, seq, d_model = x.shape
    depth = w_in.shape[0]
    assert d_model == D_MODEL and w_in.shape[2] == IN_COLS
    assert seq % SUPER == 0 and seq % TM_IN == 0 and seq % TQ_B == 0
    m = bsz * seq
    slopes = _alibi_slopes()
    slopes_b = jnp.asarray(slopes[:B_Q_HEADS])
    slopes_a = jnp.asarray(slopes[B_Q_HEADS:])
    scale = HEAD_DIM ** -0.5

    x2 = x.reshape(m, D_MODEL)
    for l in range(depth):
        qk_gains = jnp.stack([q_norm_a[l] * scale, k_norm_a[l], q_norm_b[l] * scale, k_norm_b[l]])
        a0, a1, a2, qb, kvb, gates = _in_projection(
            x2, norm1_g[l].reshape(1, D_MODEL), w_in[l].astype(BF16), qk_gains, bsz, seq)
        oa = _mixer_a(a0, a1, a2, slopes_a, bsz, seq)
        ob = _mixer_b(qb, kvb, slopes_b, sinks_b[l], bsz, seq)
        merged = _merge(oa, ob, gates, w_branch_a[l].astype(BF16), w_branch_b[l].astype(BF16))
        x2 = _ffn(x2, merged, w_out[l].astype(BF16), norm2_g[l].reshape(1, D_MODEL),
                  w_ff1[l].astype(BF16), w_ff2[l].astype(BF16))
    return x2.reshape(bsz, seq, D_MODEL)
```

```python
import functools

import numpy as np
import jax
import jax.numpy as jnp
from jax import lax
from jax.experimental import pallas as pl
from jax.experimental.pallas import tpu as pltpu

D_MODEL = 2048
HEAD_DIM = 128
BLOCK = 128
EPS = 1e-6
DILATIONS = (1, 4, 16)
A_MAX_BACK = 128
N_GROUPS = 3
A_HPG = 8
A_WIDTH = N_GROUPS * A_HPG * HEAD_DIM
B_Q_HEADS = 16
B_KV_HEADS = 2
B_GROUP = B_Q_HEADS // B_KV_HEADS
B_MAX_BACK = 127
D_FF = 4 * D_MODEL
N_ATTN_HEADS = B_Q_HEADS + N_GROUPS * A_HPG
IN_COLS = 3 * A_WIDTH + B_Q_HEADS * HEAD_DIM + 2 * B_KV_HEADS * HEAD_DIM + 2 * D_MODEL

SUPER = BLOCK * DILATIONS[-1]
NEG_INF = float("-inf")
BF16 = jnp.bfloat16
F32 = jnp.float32

VMEM_LIMIT = 56 * 1024 * 1024

TM_IN = 1024
TN_IN = 512
HEADS_PER_TILE = TN_IN // HEAD_DIM
A_TILES = 3 * A_WIDTH // TN_IN
TILES_PER_ROLE = A_WIDTH // TN_IN
TILES_PER_GROUP = A_HPG * HEAD_DIM // TN_IN
QB_TILE0 = A_TILES
QB_TILES = B_Q_HEADS * HEAD_DIM // TN_IN
KVB_TILE = QB_TILE0 + QB_TILES
GATE_TILE0 = KVB_TILE + 1
GATE_TILES = 2 * D_MODEL // TN_IN
N_TILES_IN = GATE_TILE0 + GATE_TILES

TQ_B = 1024
TM_MERGE = 512
TM_FFN = 512
TF_FFN = 512


def _alibi_slopes():
    i = np.arange(1, N_ATTN_HEADS + 1, dtype=np.float32)
    return (2.0 ** (-8.0 * i / N_ATTN_HEADS)).astype(np.float32)


def _rms_scale(a, gain):
    ms = jnp.mean(a * a, axis=-1, keepdims=True)
    return a * lax.rsqrt(ms + EPS) * gain


def _dot(a, b):
    return jnp.dot(a, b, preferred_element_type=F32)


def _dot_nt(a, b):
    return lax.dot_general(a, b, (((1,), (1,)), ((), ())), preferred_element_type=F32)


def _inproj_kernel(x_ref, g1_ref, w_ref, qkg_ref, a0_ref, a1_ref, a2_ref, qb_ref, kvb_ref,
                   gate_ref, h_scr, acc_scr):
    j = pl.program_id(1)
    t = j - 1
    tm = x_ref.shape[0]

    def matmul_tile():
        acc = _dot(h_scr[...], w_ref[...])
        for c in range(HEADS_PER_TILE):
            acc_scr[c] = acc[:, c * HEAD_DIM:(c + 1) * HEAD_DIM]

    @pl.when(j == 0)
    def _():
        rows = 256
        for c in range(tm // rows):
            x = x_ref[c * rows:(c + 1) * rows, :]
            h_scr[c * rows:(c + 1) * rows, :] = _rms_scale(x, g1_ref[...]).astype(BF16)
        matmul_tile()

    role = t // TILES_PER_ROLE
    group = (t % TILES_PER_ROLE) // TILES_PER_GROUP
    is_a = (t >= 0) & (t < A_TILES)

    def write_a(out_ref, d, normed):
        gain_a = jnp.where(role == 0, qkg_ref[0:1, :], qkg_ref[1:2, :])
        for c in range(HEADS_PER_TILE):
            for res in range(d):
                if d == 1:
                    a = acc_scr[c]
                else:
                    a = acc_scr[c, pl.ds(res, tm // d, stride=d), :]
                if normed:
                    a = _rms_scale(a, gain_a)
                if d == 1:
                    out_ref[c] = a.astype(BF16)
                else:
                    out_ref[c, res] = a.astype(BF16)

    for g, (d, out_ref) in enumerate(zip(DILATIONS, (a0_ref, a1_ref, a2_ref))):
        @pl.when(is_a & (group == g) & (role < 2))
        def _(out_ref=out_ref, d=d):
            write_a(out_ref, d, True)
            matmul_tile()

        @pl.when(is_a & (group == g) & (role == 2))
        def _(out_ref=out_ref, d=d):
            write_a(out_ref, d, False)
            matmul_tile()

    @pl.when((t >= QB_TILE0) & (t < KVB_TILE))
    def _():
        for c in range(HEADS_PER_TILE):
            qb_ref[c] = _rms_scale(acc_scr[c], qkg_ref[2:3, :]).astype(BF16)
        matmul_tile()

    @pl.when(t == KVB_TILE)
    def _():
        for c in range(B_KV_HEADS):
            kvb_ref[c] = _rms_scale(acc_scr[c], qkg_ref[3:4, :]).astype(BF16)
        for c in range(B_KV_HEADS, 2 * B_KV_HEADS):
            kvb_ref[c] = acc_scr[c].astype(BF16)
        matmul_tile()

    def write_gates():
        for c in range(HEADS_PER_TILE):
            z = acc_scr[c]
            gate_ref[:, c * HEAD_DIM:(c + 1) * HEAD_DIM] = (0.5 + 0.5 * jnp.tanh(0.5 * z)).astype(BF16)

    @pl.when((t >= GATE_TILE0) & (j < N_TILES_IN))
    def _():
        write_gates()
        matmul_tile()

    @pl.when(j == N_TILES_IN)
    def _():
        write_gates()


def _in_projection(x2, g1, w_in, qk_gains, bsz, seq):
    m = bsz * seq
    tm = TM_IN
    tiles_per_seq = seq // tm
    grid = (m // tm, N_TILES_IN + 1)

    def tile_of(j):
        return jnp.clip(j - 1, 0, N_TILES_IN - 1)

    def role_of(t):
        return jnp.clip(t // TILES_PER_ROLE, 0, 2)

    def a_map(g):
        def hb(t):
            return jnp.clip(t - TILES_PER_ROLE * role_of(t) - TILES_PER_GROUP * g, 0, TILES_PER_GROUP - 1)
        if g == 0:
            return lambda i, j: (role_of(tile_of(j)), hb(tile_of(j)), i, 0)
        return lambda i, j: (role_of(tile_of(j)), hb(tile_of(j)), i // tiles_per_seq, 0, i % tiles_per_seq, 0)

    out_shapes = [
        jax.ShapeDtypeStruct((3, A_HPG, m, HEAD_DIM), BF16),
        jax.ShapeDtypeStruct((3, A_HPG, bsz, 4, seq // 4, HEAD_DIM), BF16),
        jax.ShapeDtypeStruct((3, A_HPG, bsz, 16, seq // 16, HEAD_DIM), BF16),
        jax.ShapeDtypeStruct((B_Q_HEADS, m, HEAD_DIM), BF16),
        jax.ShapeDtypeStruct((2 * B_KV_HEADS, m, HEAD_DIM), BF16),
        jax.ShapeDtypeStruct((m, 2 * D_MODEL), BF16),
    ]
    out_specs = [
        pl.BlockSpec((None, HEADS_PER_TILE, tm, HEAD_DIM), a_map(0)),
        pl.BlockSpec((None, HEADS_PER_TILE, None, 4, tm // 4, HEAD_DIM), a_map(1)),
        pl.BlockSpec((None, HEADS_PER_TILE, None, 16, tm // 16, HEAD_DIM), a_map(2)),
        pl.BlockSpec((HEADS_PER_TILE, tm, HEAD_DIM),
                     lambda i, j: (jnp.clip(tile_of(j) - QB_TILE0, 0, QB_TILES - 1), i, 0)),
        pl.BlockSpec((2 * B_KV_HEADS, tm, HEAD_DIM), lambda i, j: (0, i, 0)),
        pl.BlockSpec((tm, TN_IN), lambda i, j: (i, jnp.clip(tile_of(j) - GATE_TILE0, 0, GATE_TILES - 1))),
    ]
    in_specs = [
        pl.BlockSpec((tm, D_MODEL), lambda i, j: (i, 0)),
        pl.BlockSpec((1, D_MODEL), lambda i, j: (0, 0)),
        pl.BlockSpec((D_MODEL, TN_IN), lambda i, j: (0, jnp.minimum(j, N_TILES_IN - 1))),
        pl.BlockSpec((4, HEAD_DIM), lambda i, j: (0, 0)),
    ]
    return pl.pallas_call(
        _inproj_kernel,
        grid=grid,
        in_specs=in_specs,
        out_specs=out_specs,
        out_shape=out_shapes,
        scratch_shapes=[pltpu.VMEM((tm, D_MODEL), BF16),
                        pltpu.VMEM((HEADS_PER_TILE, tm, HEAD_DIM), F32)],
        compiler_params=pltpu.CompilerParams(
            dimension_semantics=("arbitrary", "arbitrary"), vmem_limit_bytes=VMEM_LIMIT),
        name="in_projection",
    )(x2, g1, w_in, qk_gains)


def _band_bias(slope_times_stride, max_back):
    qi = lax.broadcasted_iota(jnp.int32, (BLOCK, 2 * BLOCK), 0) + BLOCK
    ki = lax.broadcasted_iota(jnp.int32, (BLOCK, 2 * BLOCK), 1)
    rel = qi - ki
    valid = (rel >= 0) & (rel <= max_back)
    bias = jnp.where(valid, -slope_times_stride * rel.astype(F32), NEG_INF)
    return bias, ki


def _attn_units(units):
    scores = [_dot_nt(q, kw) + bias for q, kw, _, bias in units]
    maxes = [jnp.max(s, axis=-1, keepdims=True) for s in scores]
    probs = [jnp.exp(s - m) for s, m in zip(scores, maxes)]
    sums = [jnp.sum(p, axis=-1, keepdims=True) for p in probs]
    outs = [_dot(p.astype(BF16), u[2]) for p, u in zip(probs, units)]
    return [(o / l, jnp.broadcast_to(m + jnp.log(l), (BLOCK, HEAD_DIM)))
            for o, l, m in zip(outs, sums, maxes)]


UNITS_PER_BATCH = 8


def _mixer_a_kernel(slopes_ref,
                    q0_ref, k0c_ref, k0p_ref, v0c_ref, v0p_ref,
                    q1_ref, k1c_ref, k1p_ref, v1c_ref, v1p_ref,
                    q2_ref, k2c_ref, k2p_ref, v2c_ref, v2p_ref,
                    o_ref, o1_scr, l1_scr, o2_scr, l2_scr):
    first = pl.program_id(1) == 0
    h = pl.program_id(2)

    biases = []
    for g, d in enumerate(DILATIONS):
        bias, ki = _band_bias(slopes_ref[g * A_HPG + h] * float(d), A_MAX_BACK)
        bias_head = jnp.where(first, jnp.where(ki < BLOCK, NEG_INF, bias), bias)
        biases.append((bias, bias_head))

    def window(cur_ref, prev_ref, r, blk):
        if blk == 0:
            cur = cur_ref[0:BLOCK] if r is None else cur_ref[r, 0:BLOCK]
            prev = prev_ref[...] if r is None else prev_ref[r]
            return jnp.concatenate([prev, cur], axis=0)
        lo = (blk - 1) * BLOCK
        return cur_ref[lo:lo + 2 * BLOCK] if r is None else cur_ref[r, lo:lo + 2 * BLOCK]

    def body2(it, carry):
        rs = [it * UNITS_PER_BATCH + u for u in range(UNITS_PER_BATCH)]
        units = [(q2_ref[r], window(k2c_ref, k2p_ref, r, 0), window(v2c_ref, v2p_ref, r, 0), biases[2][1])
                 for r in rs]
        for r, (o, lse) in zip(rs, _attn_units(units)):
            o2_scr[pl.ds(r, BLOCK, stride=16), :] = o
            l2_scr[pl.ds(r, BLOCK, stride=16), :] = lse
        return carry

    lax.fori_loop(0, 16 // UNITS_PER_BATCH, body2, 0)

    def body1(it, carry):
        per = UNITS_PER_BATCH // 4
        rb = [(it * per + u // 4, u % 4) for u in range(UNITS_PER_BATCH)]
        units = [(q1_ref[r, blk * BLOCK:(blk + 1) * BLOCK], window(k1c_ref, k1p_ref, r, blk),
                  window(v1c_ref, v1p_ref, r, blk), biases[1][1] if blk == 0 else biases[1][0])
                 for r, blk in rb]
        for (r, blk), (o, lse) in zip(rb, _attn_units(units)):
            rows = pl.ds(blk * BLOCK * 4 + r, BLOCK, stride=4)
            o1_scr[rows, :] = o
            l1_scr[rows, :] = lse
        return carry

    lax.fori_loop(0, 16 // UNITS_PER_BATCH, body1, 0)

    for b0 in range(0, SUPER // BLOCK, UNITS_PER_BATCH):
        blks = list(range(b0, b0 + UNITS_PER_BATCH))
        units = [(q0_ref[blk * BLOCK:(blk + 1) * BLOCK, :], window(k0c_ref, k0p_ref, None, blk),
                  window(v0c_ref, v0p_ref, None, blk), biases[0][1] if blk == 0 else biases[0][0])
                 for blk in blks]
        for blk, (o0, lse0) in zip(blks, _attn_units(units)):
            rows = slice(blk * BLOCK, (blk + 1) * BLOCK)
            lse1 = l1_scr[rows, :]
            lse2 = l2_scr[rows, :]
            mx = jnp.maximum(jnp.maximum(lse0, lse1), lse2)
            w0 = jnp.exp(lse0 - mx)
            w1 = jnp.exp(lse1 - mx)
            w2 = jnp.exp(lse2 - mx)
            o = (w0 * o0 + w1 * o1_scr[rows, :] + w2 * o2_scr[rows, :]) / (w0 + w1 + w2)
            o_ref[rows, :] = o.astype(BF16)


def _mixer_a(a0, a1, a2, slopes_a, bsz, seq):
    m = bsz * seq
    nsb = seq // SUPER
    grid = (bsz, nsb, A_HPG)

    def nat(role):
        return pl.BlockSpec((None, None, SUPER, HEAD_DIM), lambda b, s, h: (role, h, b * nsb + s, 0))

    def nat_prev(role):
        per = SUPER // BLOCK
        return pl.BlockSpec((None, None, BLOCK, HEAD_DIM),
                            lambda b, s, h: (role, h, jnp.maximum((b * nsb + s) * per - 1, 0), 0))

    def dil(role, d):
        return pl.BlockSpec((None, None, None, d, SUPER // d, HEAD_DIM),
                            lambda b, s, h: (role, h, b, 0, s, 0))

    def dil_prev(role, d):
        per = SUPER // d // BLOCK
        return pl.BlockSpec((None, None, None, d, BLOCK, HEAD_DIM),
                            lambda b, s, h: (role, h, b, 0, jnp.maximum(s * per - 1, 0), 0))

    in_specs = [pl.BlockSpec(memory_space=pltpu.SMEM),
                nat(0), nat(1), nat_prev(1), nat(2), nat_prev(2),
                dil(0, 4), dil(1, 4), dil_prev(1, 4), dil(2, 4), dil_prev(2, 4),
                dil(0, 16), dil(1, 16), dil_prev(1, 16), dil(2, 16), dil_prev(2, 16)]
    return pl.pallas_call(
        _mixer_a_kernel,
        grid=grid,
        in_specs=in_specs,
        out_specs=pl.BlockSpec((None, SUPER, HEAD_DIM), lambda b, s, h: (h, b * nsb + s, 0)),
        out_shape=jax.ShapeDtypeStruct((A_HPG, m, HEAD_DIM), BF16),
        scratch_shapes=[pltpu.VMEM((SUPER, HEAD_DIM), F32) for _ in range(4)],
        compiler_params=pltpu.CompilerParams(
            dimension_semantics=("arbitrary", "arbitrary", "arbitrary"), vmem_limit_bytes=VMEM_LIMIT),
        name="mixer_a",
    )(slopes_a, a0, a0, a0, a0, a0, a1, a1, a1, a1, a1, a2, a2, a2, a2, a2)


def _mixer_b_kernel(slopes_ref, sinks_ref, q_ref, kc_ref, kp_ref, vc_ref, vp_ref, o_ref, *, tiles_per_seq):
    first = (pl.program_id(0) % tiles_per_seq) == 0
    kvh = pl.program_id(1)
    tq = kc_ref.shape[0]

    biases, biases_head = [], []
    for hh in range(B_GROUP):
        bias, ki = _band_bias(slopes_ref[kvh * B_GROUP + hh], B_MAX_BACK)
        biases.append(bias)
        biases_head.append(jnp.where(first, jnp.where(ki < BLOCK, NEG_INF, bias), bias))

    for blk in range(tq // BLOCK):
        if blk == 0:
            kw = jnp.concatenate([kp_ref[...], kc_ref[0:BLOCK]], axis=0)
            vw = jnp.concatenate([vp_ref[...], vc_ref[0:BLOCK]], axis=0)
        else:
            kw = kc_ref[(blk - 1) * BLOCK:(blk + 1) * BLOCK]
            vw = vc_ref[(blk - 1) * BLOCK:(blk + 1) * BLOCK]
        q = q_ref[:, blk * BLOCK:(blk + 1) * BLOCK, :].reshape(B_GROUP * BLOCK, HEAD_DIM)
        s_all = _dot_nt(q, kw)
        ps, denoms = [], []
        for hh in range(B_GROUP):
            s = s_all[hh * BLOCK:(hh + 1) * BLOCK] + (biases_head[hh] if blk == 0 else biases[hh])
            sink = sinks_ref[kvh * B_GROUP + hh]
            mrow = jnp.maximum(jnp.max(s, axis=-1, keepdims=True), sink)
            p = jnp.exp(s - mrow)
            denoms.append(jnp.sum(p, axis=-1, keepdims=True) + jnp.exp(sink - mrow))
            ps.append(p.astype(BF16))
        o_all = _dot(jnp.concatenate(ps, axis=0), vw)
        for hh in range(B_GROUP):
            o = o_all[hh * BLOCK:(hh + 1) * BLOCK] / denoms[hh]
            o_ref[hh, blk * BLOCK:(blk + 1) * BLOCK, :] = o.astype(BF16)


def _mixer_b(qb, kvb, slopes_b, sinks, bsz, seq):
    m = bsz * seq
    tq = TQ_B
    per = tq // BLOCK
    grid = (m // tq, B_KV_HEADS)

    def cur(off):
        return pl.BlockSpec((None, tq, HEAD_DIM), lambda i, c: (c + off, i, 0))

    def prev(off):
        return pl.BlockSpec((None, BLOCK, HEAD_DIM), lambda i, c: (c + off, jnp.maximum(i * per - 1, 0), 0))

    smem = pl.BlockSpec(memory_space=pltpu.SMEM)
    return pl.pallas_call(
        functools.partial(_mixer_b_kernel, tiles_per_seq=seq // tq),
        grid=grid,
        in_specs=[smem, smem,
                  pl.BlockSpec((B_GROUP, tq, HEAD_DIM), lambda i, c: (c, i, 0)),
                  cur(0), prev(0), cur(B_KV_HEADS), prev(B_KV_HEADS)],
        out_specs=pl.BlockSpec((B_GROUP, tq, HEAD_DIM), lambda i, c: (c, i, 0)),
        out_shape=jax.ShapeDtypeStruct((B_Q_HEADS, m, HEAD_DIM), BF16),
        compiler_params=pltpu.CompilerParams(
            dimension_semantics=("arbitrary", "arbitrary"), vmem_limit_bytes=VMEM_LIMIT),
        name="mixer_b",
    )(slopes_b, sinks, qb, kvb, kvb, kvb, kvb)


def _merge_kernel(oa_ref, ob_ref, gate_ref, wa_ref, wb_ref, out_ref):
    oa = jnp.concatenate([oa_ref[hh] for hh in range(A_HPG)], axis=-1)
    ob = jnp.concatenate([ob_ref[hh] for hh in range(B_Q_HEADS)], axis=-1)
    ga = gate_ref[:, 0:D_MODEL].astype(F32)
    gb = gate_ref[:, D_MODEL:2 * D_MODEL].astype(F32)
    merged = ga * _dot(oa, wa_ref[...]) + gb * _dot(ob, wb_ref[...])
    out_ref[...] = merged.astype(BF16)


def _merge(oa, ob, gates, wa, wb):
    m = gates.shape[0]
    tm = TM_MERGE
    resident = dict(pipeline_mode=pl.Buffered(1))
    return pl.pallas_call(
        _merge_kernel,
        grid=(m // tm,),
        in_specs=[pl.BlockSpec((A_HPG, tm, HEAD_DIM), lambda i: (0, i, 0)),
                  pl.BlockSpec((B_Q_HEADS, tm, HEAD_DIM), lambda i: (0, i, 0)),
                  pl.BlockSpec((tm, 2 * D_MODEL), lambda i: (i, 0)),
                  pl.BlockSpec(wa.shape, lambda i: (0, 0), **resident),
                  pl.BlockSpec(wb.shape, lambda i: (0, 0), **resident)],
        out_specs=pl.BlockSpec((tm, D_MODEL), lambda i: (i, 0)),
        out_shape=jax.ShapeDtypeStruct((m, D_MODEL), BF16),
        compiler_params=pltpu.CompilerParams(
            dimension_semantics=("arbitrary",), vmem_limit_bytes=VMEM_LIMIT),
        name="gated_merge",
    )(oa, ob, gates, wa, wb)


def _ffn_kernel(x_ref, mg_ref, wo_ref, g2_ref, w1_ref, w2_ref, out_ref, h2_scr):
    @pl.when(pl.program_id(1) == 0)
    def _():
        x1 = x_ref[...] + _dot(mg_ref[...], wo_ref[...])
        out_ref[...] = x1
        h2_scr[...] = _rms_scale(x1, g2_ref[...]).astype(BF16)

    z = _dot(h2_scr[...], w1_ref[...])
    act = jnp.square(jnp.maximum(z, 0.0)).astype(BF16)
    out_ref[...] += _dot(act, w2_ref[...])


def _ffn(x2, merged, wo, g2, w1, w2):
    m = x2.shape[0]
    tm, tf = TM_FFN, TF_FFN
    return pl.pallas_call(
        _ffn_kernel,
        grid=(m // tm, D_FF // tf),
        in_specs=[pl.BlockSpec((tm, D_MODEL), lambda i, f: (i, 0)),
                  pl.BlockSpec((tm, D_MODEL), lambda i, f: (i, 0)),
                  pl.BlockSpec(wo.shape, lambda i, f: (0, 0), pipeline_mode=pl.Buffered(1)),
                  pl.BlockSpec((1, D_MODEL), lambda i, f: (0, 0)),
                  pl.BlockSpec((D_MODEL, tf), lambda i, f: (0, f)),
                  pl.BlockSpec((tf, D_MODEL), lambda i, f: (f, 0))],
        out_specs=pl.BlockSpec((tm, D_MODEL), lambda i, f: (i, 0)),
        out_shape=jax.ShapeDtypeStruct((m, D_MODEL), F32),
        scratch_shapes=[pltpu.VMEM((tm, D_MODEL), BF16)],
        compiler_params=pltpu.CompilerParams(
            dimension_semantics=("arbitrary", "arbitrary"), vmem_limit_bytes=VMEM_LIMIT),
        name="out_proj_ffn",
    )(x2, merged, wo, g2, w1, w2)


def kernel(x, norm1_g, w_in, q_norm_a, k_norm_a, q_norm_b, k_norm_b, sinks_b, w_branch_a, w_branch_b,
           w_out, norm2_g, w_ff1, w_ff2):
    bsz, seq, d_model = x.shape
    depth = w_in.shape[0]
    assert d_model == D_MODEL and w_in.shape[2] == IN_COLS
    assert seq % SUPER == 0 and seq % TM_IN == 0 and seq % TQ_B == 0
    m = bsz * seq
    slopes = _alibi_slopes()
    slopes_b = jnp.asarray(slopes[:B_Q_HEADS])
    slopes_a = jnp.asarray(slopes[B_Q_HEADS:])
    scale = HEAD_DIM ** -0.5

    x2 = x.reshape(m, D_MODEL)
    for l in range(depth):
        qk_gains = jnp.stack([q_norm_a[l] * scale, k_norm_a[l], q_norm_b[l] * scale, k_norm_b[l]])
        a0, a1, a2, qb, kvb, gates = _in_projection(
            x2, norm1_g[l].reshape(1, D_MODEL), w_in[l].astype(BF16), qk_gains, bsz, seq)
        oa = _mixer_a(a0, a1, a2, slopes_a, bsz, seq)
        ob = _mixer_b(qb, kvb, slopes_b, sinks_b[l], bsz, seq)
        merged = _merge(oa, ob, gates, w_branch_a[l].astype(BF16), w_branch_b[l].astype(BF16))
        x2 = _ffn(x2, merged, w_out[l].astype(BF16), norm2_g[l].reshape(1, D_MODEL),
                  w_ff1[l].astype(BF16), w_ff2[l].astype(BF16))
    return x2.reshape(bsz, seq, D_MODEL)
```

```python
import functools

import numpy as np
import jax
import jax.numpy as jnp
from jax import lax
from jax.experimental import pallas as pl
from jax.experimental.pallas import tpu as pltpu

D_MODEL = 2048
HEAD_DIM = 128
BLOCK = 128
EPS = 1e-6
DILATIONS = (1, 4, 16)
A_MAX_BACK = 128
N_GROUPS = 3
A_HPG = 8
A_WIDTH = N_GROUPS * A_HPG * HEAD_DIM
B_Q_HEADS = 16
B_KV_HEADS = 2
B_GROUP = B_Q_HEADS // B_KV_HEADS
B_MAX_BACK = 127
D_FF = 4 * D_MODEL
N_ATTN_HEADS = B_Q_HEADS + N_GROUPS * A_HPG
IN_COLS = 3 * A_WIDTH + B_Q_HEADS * HEAD_DIM + 2 * B_KV_HEADS * HEAD_DIM + 2 * D_MODEL

SUPER = BLOCK * DILATIONS[-1]
NEG_INF = float("-inf")
LOG2E = 1.4426950408889634
BF16 = jnp.bfloat16
F32 = jnp.float32

VMEM_LIMIT = 56 * 1024 * 1024

TM_IN = 1024
TN_IN = 512
HEADS_PER_TILE = TN_IN // HEAD_DIM
A_TILES = 3 * A_WIDTH // TN_IN
TILES_PER_ROLE = A_WIDTH // TN_IN
TILES_PER_GROUP = A_HPG * HEAD_DIM // TN_IN
QB_TILE0 = A_TILES
QB_TILES = B_Q_HEADS * HEAD_DIM // TN_IN
KVB_TILE = QB_TILE0 + QB_TILES
GATE_TILE0 = KVB_TILE + 1
GATE_TILES = 2 * D_MODEL // TN_IN
N_TILES_IN = GATE_TILE0 + GATE_TILES

TQ_B = 1024
TM_MERGE = 512
TM_FFN = 512
TF_FFN = 1024


def _alibi_slopes():
    i = np.arange(1, N_ATTN_HEADS + 1, dtype=np.float32)
    return (2.0 ** (-8.0 * i / N_ATTN_HEADS)).astype(np.float32)


def _rms_scale(a, gain):
    ms = jnp.mean(a * a, axis=-1, keepdims=True)
    return a * lax.rsqrt(ms + EPS) * gain


def _dot(a, b):
    return jnp.dot(a, b, preferred_element_type=F32)


def _dot_nt(a, b):
    return lax.dot_general(a, b, (((1,), (1,)), ((), ())), preferred_element_type=F32)


def _inproj_kernel(x_ref, g1_ref, w_ref, qkg_ref, a0_ref, a1_ref, a2_ref, qb_ref, kvb_ref,
                   gate_ref, h_scr, acc_scr):
    j = pl.program_id(1)
    t = j - 1
    tm = x_ref.shape[0]

    def matmul_tile():
        acc = _dot(h_scr[...], w_ref[...].astype(BF16))
        for c in range(HEADS_PER_TILE):
            acc_scr[c] = acc[:, c * HEAD_DIM:(c + 1) * HEAD_DIM]

    @pl.when(j == 0)
    def _():
        rows = 256
        for c in range(tm // rows):
            x = x_ref[c * rows:(c + 1) * rows, :]
            h_scr[c * rows:(c + 1) * rows, :] = _rms_scale(x, g1_ref[...]).astype(BF16)
        matmul_tile()

    role = t // TILES_PER_ROLE
    group = (t % TILES_PER_ROLE) // TILES_PER_GROUP
    is_a = (t >= 0) & (t < A_TILES)

    def write_a(out_ref, d, normed):
        gain_a = jnp.where(role == 0, qkg_ref[0:1, :], qkg_ref[1:2, :])
        for c in range(HEADS_PER_TILE):
            for res in range(d):
                if d == 1:
                    a = acc_scr[c]
                else:
                    a = acc_scr[c, pl.ds(res, tm // d, stride=d), :]
                if normed:
                    a = _rms_scale(a, gain_a)
                if d == 1:
                    out_ref[c] = a.astype(BF16)
                else:
                    out_ref[c, res] = a.astype(BF16)

    for g, (d, out_ref) in enumerate(zip(DILATIONS, (a0_ref, a1_ref, a2_ref))):
        @pl.when(is_a & (group == g) & (role < 2))
        def _(out_ref=out_ref, d=d):
            write_a(out_ref, d, True)
            matmul_tile()

        @pl.when(is_a & (group == g) & (role == 2))
        def _(out_ref=out_ref, d=d):
            write_a(out_ref, d, False)
            matmul_tile()

    @pl.when((t >= QB_TILE0) & (t < KVB_TILE))
    def _():
        for c in range(HEADS_PER_TILE):
            qb_ref[c] = _rms_scale(acc_scr[c], qkg_ref[2:3, :]).astype(BF16)
        matmul_tile()

    @pl.when(t == KVB_TILE)
    def _():
        for c in range(B_KV_HEADS):
            kvb_ref[c] = _rms_scale(acc_scr[c], qkg_ref[3:4, :]).astype(BF16)
        for c in range(B_KV_HEADS, 2 * B_KV_HEADS):
            kvb_ref[c] = acc_scr[c].astype(BF16)
        matmul_tile()

    def write_gates():
        for c in range(HEADS_PER_TILE):
            z = acc_scr[c]
            gate_ref[:, c * HEAD_DIM:(c + 1) * HEAD_DIM] = (0.5 + 0.5 * jnp.tanh(0.5 * z)).astype(BF16)

    @pl.when((t >= GATE_TILE0) & (j < N_TILES_IN))
    def _():
        write_gates()
        matmul_tile()

    @pl.when(j == N_TILES_IN)
    def _():
        write_gates()


def _in_projection(x2, g1, w_in, qk_gains, bsz, seq):
    m = bsz * seq
    tm = TM_IN
    tiles_per_seq = seq // tm
    grid = (m // tm, N_TILES_IN + 1)

    def tile_of(j):
        return jnp.clip(j - 1, 0, N_TILES_IN - 1)

    def role_of(t):
        return jnp.clip(t // TILES_PER_ROLE, 0, 2)

    def a_map(g):
        def hb(t):
            return jnp.clip(t - TILES_PER_ROLE * role_of(t) - TILES_PER_GROUP * g, 0, TILES_PER_GROUP - 1)
        if g == 0:
            return lambda i, j: (role_of(tile_of(j)), hb(tile_of(j)), i, 0)
        return lambda i, j: (role_of(tile_of(j)), hb(tile_of(j)), i // tiles_per_seq, 0, i % tiles_per_seq, 0)

    out_shapes = [
        jax.ShapeDtypeStruct((3, A_HPG, m, HEAD_DIM), BF16),
        jax.ShapeDtypeStruct((3, A_HPG, bsz, 4, seq // 4, HEAD_DIM), BF16),
        jax.ShapeDtypeStruct((3, A_HPG, bsz, 16, seq // 16, HEAD_DIM), BF16),
        jax.ShapeDtypeStruct((B_Q_HEADS, m, HEAD_DIM), BF16),
        jax.ShapeDtypeStruct((2 * B_KV_HEADS, m, HEAD_DIM), BF16),
        jax.ShapeDtypeStruct((m, 2 * D_MODEL), BF16),
    ]
    out_specs = [
        pl.BlockSpec((None, HEADS_PER_TILE, tm, HEAD_DIM), a_map(0)),
        pl.BlockSpec((None, HEADS_PER_TILE, None, 4, tm // 4, HEAD_DIM), a_map(1)),
        pl.BlockSpec((None, HEADS_PER_TILE, None, 16, tm // 16, HEAD_DIM), a_map(2)),
        pl.BlockSpec((HEADS_PER_TILE, tm, HEAD_DIM),
                     lambda i, j: (jnp.clip(tile_of(j) - QB_TILE0, 0, QB_TILES - 1), i, 0)),
        pl.BlockSpec((2 * B_KV_HEADS, tm, HEAD_DIM), lambda i, j: (0, i, 0)),
        pl.BlockSpec((tm, TN_IN), lambda i, j: (i, jnp.clip(tile_of(j) - GATE_TILE0, 0, GATE_TILES - 1))),
    ]
    in_specs = [
        pl.BlockSpec((tm, D_MODEL), lambda i, j: (i, 0)),
        pl.BlockSpec((1, D_MODEL), lambda i, j: (0, 0)),
        pl.BlockSpec((D_MODEL, TN_IN), lambda i, j: (0, jnp.minimum(j, N_TILES_IN - 1))),
        pl.BlockSpec((4, HEAD_DIM), lambda i, j: (0, 0)),
    ]
    return pl.pallas_call(
        _inproj_kernel,
        grid=grid,
        in_specs=in_specs,
        out_specs=out_specs,
        out_shape=out_shapes,
        scratch_shapes=[pltpu.VMEM((tm, D_MODEL), BF16),
                        pltpu.VMEM((HEADS_PER_TILE, tm, HEAD_DIM), F32)],
        compiler_params=pltpu.CompilerParams(
            dimension_semantics=("arbitrary", "arbitrary"), vmem_limit_bytes=VMEM_LIMIT),
        name="in_projection",
    )(x2, g1, w_in, qk_gains)


def _band_bias(slope_times_stride, max_back):
    qi = lax.broadcasted_iota(jnp.int32, (BLOCK, 2 * BLOCK), 0) + BLOCK
    ki = lax.broadcasted_iota(jnp.int32, (BLOCK, 2 * BLOCK), 1)
    rel = qi - ki
    valid = (rel >= 0) & (rel <= max_back)
    bias = jnp.where(valid, -slope_times_stride * rel.astype(F32), NEG_INF)
    return bias, ki


def _with_ones(vw):
    return jnp.concatenate([vw, jnp.ones(vw.shape, vw.dtype)], axis=-1)


def _attn_units(units):
    scores = [_dot_nt(q, kw) + bias for q, kw, _, bias in units]
    maxes = [jnp.max(s, axis=-1, keepdims=True) for s in scores]
    probs = [jnp.exp2(s - m).astype(BF16) for s, m in zip(scores, maxes)]
    outs = [_dot(p, _with_ones(u[2])) for p, u in zip(probs, units)]
    results = []
    for o_l, m in zip(outs, maxes):
        o, l = o_l[:, :HEAD_DIM], o_l[:, HEAD_DIM:]
        results.append((o / l, m + jnp.log2(l)))
    return results


UNITS_PER_BATCH = 8


def _mixer_a_kernel(slopes_ref,
                    q0_ref, k0c_ref, k0p_ref, v0c_ref, v0p_ref,
                    q1_ref, k1c_ref, k1p_ref, v1c_ref, v1p_ref,
                    q2_ref, k2c_ref, k2p_ref, v2c_ref, v2p_ref,
                    o_ref, o1_scr, l1_scr, o2_scr, l2_scr):
    first = pl.program_id(1) == 0
    h = pl.program_id(2)

    biases = []
    for g, d in enumerate(DILATIONS):
        bias, ki = _band_bias(slopes_ref[g * A_HPG + h] * float(d), A_MAX_BACK)
        bias_head = jnp.where(first, jnp.where(ki < BLOCK, NEG_INF, bias), bias)
        biases.append((bias, bias_head))

    def window(cur_ref, prev_ref, r, blk):
        if blk == 0:
            cur = cur_ref[0:BLOCK] if r is None else cur_ref[r, 0:BLOCK]
            prev = prev_ref[...] if r is None else prev_ref[r]
            return jnp.concatenate([prev, cur], axis=0)
        lo = (blk - 1) * BLOCK
        return cur_ref[lo:lo + 2 * BLOCK] if r is None else cur_ref[r, lo:lo + 2 * BLOCK]

    def body2(it, carry):
        rs = [it * UNITS_PER_BATCH + u for u in range(UNITS_PER_BATCH)]
        units = [(q2_ref[r], window(k2c_ref, k2p_ref, r, 0), window(v2c_ref, v2p_ref, r, 0), biases[2][1])
                 for r in rs]
        for r, (o, lse) in zip(rs, _attn_units(units)):
            o2_scr[pl.ds(r, BLOCK, stride=16), :] = o
            l2_scr[pl.ds(r, BLOCK, stride=16), :] = lse
        return carry

    lax.fori_loop(0, 16 // UNITS_PER_BATCH, body2, 0)

    def body1(it, carry):
        per = UNITS_PER_BATCH // 4
        rb = [(it * per + u // 4, u % 4) for u in range(UNITS_PER_BATCH)]
        units = [(q1_ref[r, blk * BLOCK:(blk + 1) * BLOCK], window(k1c_ref, k1p_ref, r, blk),
                  window(v1c_ref, v1p_ref, r, blk), biases[1][1] if blk == 0 else biases[1][0])
                 for r, blk in rb]
        for (r, blk), (o, lse) in zip(rb, _attn_units(units)):
            rows = pl.ds(blk * BLOCK * 4 + r, BLOCK, stride=4)
            o1_scr[rows, :] = o
            l1_scr[rows, :] = lse
        return carry

    lax.fori_loop(0, 16 // UNITS_PER_BATCH, body1, 0)

    for b0 in range(0, SUPER // BLOCK, UNITS_PER_BATCH):
        blks = list(range(b0, b0 + UNITS_PER_BATCH))
        units = [(q0_ref[blk * BLOCK:(blk + 1) * BLOCK, :], window(k0c_ref, k0p_ref, None, blk),
                  window(v0c_ref, v0p_ref, None, blk), biases[0][1] if blk == 0 else biases[0][0])
                 for blk in blks]
        for blk, (o0, lse0) in zip(blks, _attn_units(units)):
            rows = slice(blk * BLOCK, (blk + 1) * BLOCK)
            lse1 = l1_scr[rows, :]
            lse2 = l2_scr[rows, :]
            mx = jnp.maximum(jnp.maximum(lse0, lse1), lse2)
            w0 = jnp.exp2(lse0 - mx)
            w1 = jnp.exp2(lse1 - mx)
            w2 = jnp.exp2(lse2 - mx)
            o = (w0 * o0 + w1 * o1_scr[rows, :] + w2 * o2_scr[rows, :]) / (w0 + w1 + w2)
            o_ref[rows, :] = o.astype(BF16)


def _mixer_a(a0, a1, a2, slopes_a, bsz, seq):
    m = bsz * seq
    nsb = seq // SUPER
    grid = (bsz, nsb, A_HPG)

    def nat(role):
        return pl.BlockSpec((None, None, SUPER, HEAD_DIM), lambda b, s, h: (role, h, b * nsb + s, 0))

    def nat_prev(role):
        per = SUPER // BLOCK
        return pl.BlockSpec((None, None, BLOCK, HEAD_DIM),
                            lambda b, s, h: (role, h, jnp.maximum((b * nsb + s) * per - 1, 0), 0))

    def dil(role, d):
        return pl.BlockSpec((None, None, None, d, SUPER // d, HEAD_DIM),
                            lambda b, s, h: (role, h, b, 0, s, 0))

    def dil_prev(role, d):
        per = SUPER // d // BLOCK
        return pl.BlockSpec((None, None, None, d, BLOCK, HEAD_DIM),
                            lambda b, s, h: (role, h, b, 0, jnp.maximum(s * per - 1, 0), 0))

    in_specs = [pl.BlockSpec(memory_space=pltpu.SMEM),
                nat(0), nat(1), nat_prev(1), nat(2), nat_prev(2),
                dil(0, 4), dil(1, 4), dil_prev(1, 4), dil(2, 4), dil_prev(2, 4),
                dil(0, 16), dil(1, 16), dil_prev(1, 16), dil(2, 16), dil_prev(2, 16)]
    return pl.pallas_call(
        _mixer_a_kernel,
        grid=grid,
        in_specs=in_specs,
        out_specs=pl.BlockSpec((None, SUPER, HEAD_DIM), lambda b, s, h: (h, b * nsb + s, 0)),
        out_shape=jax.ShapeDtypeStruct((A_HPG, m, HEAD_DIM), BF16),
        scratch_shapes=[pltpu.VMEM((SUPER, HEAD_DIM), F32) for _ in range(4)],
        compiler_params=pltpu.CompilerParams(
            dimension_semantics=("arbitrary", "arbitrary", "arbitrary"), vmem_limit_bytes=VMEM_LIMIT),
        name="mixer_a",
    )(slopes_a, a0, a0, a0, a0, a0, a1, a1, a1, a1, a1, a2, a2, a2, a2, a2)


def _mixer_b_kernel(slopes_ref, sinks_ref, q_ref, kc_ref, kp_ref, vc_ref, vp_ref, o_ref, *, tiles_per_seq):
    first = (pl.program_id(0) % tiles_per_seq) == 0
    kvh = pl.program_id(1)
    tq = kc_ref.shape[0]

    biases, biases_head = [], []
    for hh in range(B_GROUP):
        bias, ki = _band_bias(slopes_ref[kvh * B_GROUP + hh], B_MAX_BACK)
        biases.append(bias)
        biases_head.append(jnp.where(first, jnp.where(ki < BLOCK, NEG_INF, bias), bias))

    for blk in range(tq // BLOCK):
        if blk == 0:
            kw = jnp.concatenate([kp_ref[...], kc_ref[0:BLOCK]], axis=0)
            vw = jnp.concatenate([vp_ref[...], vc_ref[0:BLOCK]], axis=0)
        else:
            kw = kc_ref[(blk - 1) * BLOCK:(blk + 1) * BLOCK]
            vw = vc_ref[(blk - 1) * BLOCK:(blk + 1) * BLOCK]
        q = q_ref[:, blk * BLOCK:(blk + 1) * BLOCK, :].reshape(B_GROUP * BLOCK, HEAD_DIM)
        s_all = _dot_nt(q, kw)
        ps, sink_terms = [], []
        for hh in range(B_GROUP):
            s = s_all[hh * BLOCK:(hh + 1) * BLOCK] + (biases_head[hh] if blk == 0 else biases[hh])
            sink = sinks_ref[kvh * B_GROUP + hh]
            mrow = jnp.maximum(jnp.max(s, axis=-1, keepdims=True), sink)
            ps.append(jnp.exp2(s - mrow).astype(BF16))
            sink_terms.append(jnp.exp2(sink - mrow))
        o_all = _dot(jnp.concatenate(ps, axis=0), _with_ones(vw))
        for hh in range(B_GROUP):
            rows = slice(hh * BLOCK, (hh + 1) * BLOCK)
            o = o_all[rows, :HEAD_DIM] / (o_all[rows, HEAD_DIM:] + sink_terms[hh])
            o_ref[hh, blk * BLOCK:(blk + 1) * BLOCK, :] = o.astype(BF16)


def _mixer_b(qb, kvb, slopes_b, sinks, bsz, seq):
    m = bsz * seq
    tq = TQ_B
    per = tq // BLOCK
    grid = (m // tq, B_KV_HEADS)

    def cur(off):
        return pl.BlockSpec((None, tq, HEAD_DIM), lambda i, c: (c + off, i, 0))

    def prev(off):
        return pl.BlockSpec((None, BLOCK, HEAD_DIM), lambda i, c: (c + off, jnp.maximum(i * per - 1, 0), 0))

    smem = pl.BlockSpec(memory_space=pltpu.SMEM)
    return pl.pallas_call(
        functools.partial(_mixer_b_kernel, tiles_per_seq=seq // tq),
        grid=grid,
        in_specs=[smem, smem,
                  pl.BlockSpec((B_GROUP, tq, HEAD_DIM), lambda i, c: (c, i, 0)),
                  cur(0), prev(0), cur(B_KV_HEADS), prev(B_KV_HEADS)],
        out_specs=pl.BlockSpec((B_GROUP, tq, HEAD_DIM), lambda i, c: (c, i, 0)),
        out_shape=jax.ShapeDtypeStruct((B_Q_HEADS, m, HEAD_DIM), BF16),
        compiler_params=pltpu.CompilerParams(
            dimension_semantics=("arbitrary", "arbitrary"), vmem_limit_bytes=VMEM_LIMIT),
        name="mixer_b",
    )(slopes_b, sinks, qb, kvb, kvb, kvb, kvb)


def _merge_kernel(oa_ref, ob_ref, gate_ref, wa_ref, wb_ref, out_ref):
    oa = jnp.concatenate([oa_ref[hh] for hh in range(A_HPG)], axis=-1)
    ob = jnp.concatenate([ob_ref[hh] for hh in range(B_Q_HEADS)], axis=-1)
    ga = gate_ref[:, 0:D_MODEL].astype(F32)
    gb = gate_ref[:, D_MODEL:2 * D_MODEL].astype(F32)
    merged = ga * _dot(oa, wa_ref[...]) + gb * _dot(ob, wb_ref[...])
    out_ref[...] = merged.astype(BF16)


def _merge(oa, ob, gates, wa, wb):
    m = gates.shape[0]
    tm = TM_MERGE
    resident = dict(pipeline_mode=pl.Buffered(1))
    return pl.pallas_call(
        _merge_kernel,
        grid=(m // tm,),
        in_specs=[pl.BlockSpec((A_HPG, tm, HEAD_DIM), lambda i: (0, i, 0)),
                  pl.BlockSpec((B_Q_HEADS, tm, HEAD_DIM), lambda i: (0, i, 0)),
                  pl.BlockSpec((tm, 2 * D_MODEL), lambda i: (i, 0)),
                  pl.BlockSpec(wa.shape, lambda i: (0, 0), **resident),
                  pl.BlockSpec(wb.shape, lambda i: (0, 0), **resident)],
        out_specs=pl.BlockSpec((tm, D_MODEL), lambda i: (i, 0)),
        out_shape=jax.ShapeDtypeStruct((m, D_MODEL), BF16),
        compiler_params=pltpu.CompilerParams(
            dimension_semantics=("arbitrary",), vmem_limit_bytes=VMEM_LIMIT),
        name="gated_merge",
    )(oa, ob, gates, wa, wb)


def _ffn_kernel(x_ref, mg_ref, wo_ref, g2_ref, w1_ref, w2_ref, out_ref, h2_scr):
    @pl.when(pl.program_id(1) == 0)
    def _():
        x1 = x_ref[...] + _dot(mg_ref[...], wo_ref[...])
        out_ref[...] = x1
        h2_scr[...] = _rms_scale(x1, g2_ref[...]).astype(BF16)

    z = _dot(h2_scr[...], w1_ref[...])
    act = jnp.square(jnp.maximum(z, 0.0)).astype(BF16)
    out_ref[...] += _dot(act, w2_ref[...])


def _ffn(x2, merged, wo, g2, w1, w2):
    m = x2.shape[0]
    tm, tf = TM_FFN, TF_FFN
    return pl.pallas_call(
        _ffn_kernel,
        grid=(m // tm, D_FF // tf),
        in_specs=[pl.BlockSpec((tm, D_MODEL), lambda i, f: (i, 0)),
                  pl.BlockSpec((tm, D_MODEL), lambda i, f: (i, 0)),
                  pl.BlockSpec(wo.shape, lambda i, f: (0, 0), pipeline_mode=pl.Buffered(1)),
                  pl.BlockSpec((1, D_MODEL), lambda i, f: (0, 0)),
                  pl.BlockSpec((D_MODEL, tf), lambda i, f: (0, f)),
                  pl.BlockSpec((tf, D_MODEL), lambda i, f: (f, 0))],
        out_specs=pl.BlockSpec((tm, D_MODEL), lambda i, f: (i, 0)),
        out_shape=jax.ShapeDtypeStruct((m, D_MODEL), F32),
        scratch_shapes=[pltpu.VMEM((tm, D_MODEL), BF16)],
        compiler_params=pltpu.CompilerParams(
            dimension_semantics=("arbitrary", "arbitrary"), vmem_limit_bytes=VMEM_LIMIT),
        name="out_proj_ffn",
    )(x2, merged, wo, g2, w1, w2)


def kernel(x, norm1_g, w_in, q_norm_a, k_norm_a, q_norm_b, k_norm_b, sinks_b, w_branch_a, w_branch_b,
           w_out, norm2_g, w_ff1, w_ff2):
    bsz, seq, d_model = x.shape
    depth = w_in.shape[0]
    assert d_model == D_MODEL and w_in.shape[2] == IN_COLS
    assert seq % SUPER == 0 and seq % TM_IN == 0 and seq % TQ_B == 0
    m = bsz * seq
    slopes = _alibi_slopes() * np.float32(LOG2E)
    slopes_b = jnp.asarray(slopes[:B_Q_HEADS])
    slopes_a = jnp.asarray(slopes[B_Q_HEADS:])
    scale = HEAD_DIM ** -0.5 * LOG2E

    x2 = x.reshape(m, D_MODEL)
    for l in range(depth):
        qk_gains = jnp.stack([q_norm_a[l] * scale, k_norm_a[l], q_norm_b[l] * scale, k_norm_b[l]])
        a0, a1, a2, qb, kvb, gates = _in_projection(
            x2, norm1_g[l].reshape(1, D_MODEL), w_in[l], qk_gains, bsz, seq)
        oa = _mixer_a(a0, a1, a2, slopes_a, bsz, seq)
        ob = _mixer_b(qb, kvb, slopes_b, sinks_b[l] * LOG2E, bsz, seq)
        merged = _merge(oa, ob, gates, w_branch_a[l].astype(BF16), w_branch_b[l].astype(BF16))
        x2 = _ffn(x2, merged, w_out[l].astype(BF16), norm2_g[l].reshape(1, D_MODEL),
                  w_ff1[l].astype(BF16), w_ff2[l].astype(BF16))
    return x2.reshape(bsz, seq, D_MODEL)
```

```python
import functools

import numpy as np
import jax
import jax.numpy as jnp
from jax import lax
from jax.experimental import pallas as pl
from jax.experimental.pallas import tpu as pltpu

D_MODEL = 2048
HEAD_DIM = 128
BLOCK = 128
EPS = 1e-6
DILATIONS = (1, 4, 16)
A_MAX_BACK = 128
N_GROUPS = 3
A_HPG = 8
A_WIDTH = N_GROUPS * A_HPG * HEAD_DIM
B_Q_HEADS = 16
B_KV_HEADS = 2
B_GROUP = B_Q_HEADS // B_KV_HEADS
B_MAX_BACK = 127
D_FF = 4 * D_MODEL
N_ATTN_HEADS = B_Q_HEADS + N_GROUPS * A_HPG
IN_COLS = 3 * A_WIDTH + B_Q_HEADS * HEAD_DIM + 2 * B_KV_HEADS * HEAD_DIM + 2 * D_MODEL

SUPER = BLOCK * DILATIONS[-1]
NEG_INF = float("-inf")
LOG2E = 1.4426950408889634
BF16 = jnp.bfloat16
F32 = jnp.float32

VMEM_LIMIT = 56 * 1024 * 1024

TM_IN = 1024
TN_IN = 512
HEADS_PER_TILE = TN_IN // HEAD_DIM
A_TILES = 3 * A_WIDTH // TN_IN
TILES_PER_ROLE = A_WIDTH // TN_IN
TILES_PER_GROUP = A_HPG * HEAD_DIM // TN_IN
QB_TILE0 = A_TILES
QB_TILES = B_Q_HEADS * HEAD_DIM // TN_IN
KVB_TILE = QB_TILE0 + QB_TILES
GATE_TILE0 = KVB_TILE + 1
GATE_TILES = 2 * D_MODEL // TN_IN
N_TILES_IN = GATE_TILE0 + GATE_TILES

TQ_B = 1024
TM_MERGE = 512
TM_FFN = 512
TF_FFN = 1024


def _alibi_slopes():
    i = np.arange(1, N_ATTN_HEADS + 1, dtype=np.float32)
    return (2.0 ** (-8.0 * i / N_ATTN_HEADS)).astype(np.float32)


def _rms_scale(a, gain):
    ms = jnp.mean(a * a, axis=-1, keepdims=True)
    return a * lax.rsqrt(ms + EPS) * gain


def _dot(a, b):
    return jnp.dot(a, b, preferred_element_type=F32)


def _dot_nt(a, b):
    return lax.dot_general(a, b, (((1,), (1,)), ((), ())), preferred_element_type=F32)


def _inproj_kernel(x_ref, g1_ref, w_ref, qkg_ref, a0_ref, a1_ref, a2_ref, qb_ref, kvb_ref,
                   gate_ref, h_scr, acc_scr, *, n_steps):
    s = pl.program_id(0)
    j = s % N_TILES_IN
    t = jnp.maximum(s - 1, 0) % N_TILES_IN
    has_matmul = s < n_steps - 1
    tm = x_ref.shape[0]

    def matmul_tile():
        acc = _dot(h_scr[...], w_ref[...])
        for c in range(HEADS_PER_TILE):
            acc_scr[c] = acc[:, c * HEAD_DIM:(c + 1) * HEAD_DIM]

    def write_gates():
        for c in range(HEADS_PER_TILE):
            z = acc_scr[c]
            gate_ref[:, c * HEAD_DIM:(c + 1) * HEAD_DIM] = (0.5 + 0.5 * jnp.tanh(0.5 * z)).astype(BF16)

    @pl.when(has_matmul & (j == 0))
    def _():
        @pl.when(s > 0)
        def _():
            write_gates()
        rows = 256
        for c in range(tm // rows):
            x = x_ref[c * rows:(c + 1) * rows, :]
            h_scr[c * rows:(c + 1) * rows, :] = _rms_scale(x, g1_ref[...]).astype(BF16)
        matmul_tile()

    @pl.when(s == n_steps - 1)
    def _():
        write_gates()

    mid_row = has_matmul & (j > 0)
    role = t // TILES_PER_ROLE
    group = (t % TILES_PER_ROLE) // TILES_PER_GROUP
    is_a = mid_row & (t < A_TILES)

    def write_a(out_ref, d, normed):
        gain_a = jnp.where(role == 0, qkg_ref[0:1, :], qkg_ref[1:2, :])
        for c in range(HEADS_PER_TILE):
            for res in range(d):
                if d == 1:
                    a = acc_scr[c]
                else:
                    a = acc_scr[c, pl.ds(res, tm // d, stride=d), :]
                if normed:
                    a = _rms_scale(a, gain_a)
                if d == 1:
                    out_ref[c] = a.astype(BF16)
                else:
                    out_ref[c, res] = a.astype(BF16)

    for g, (d, out_ref) in enumerate(zip(DILATIONS, (a0_ref, a1_ref, a2_ref))):
        @pl.when(is_a & (group == g) & (role < 2))
        def _(out_ref=out_ref, d=d):
            write_a(out_ref, d, True)
            matmul_tile()

        @pl.when(is_a & (group == g) & (role == 2))
        def _(out_ref=out_ref, d=d):
            write_a(out_ref, d, False)
            matmul_tile()

    @pl.when(mid_row & (t >= QB_TILE0) & (t < KVB_TILE))
    def _():
        for c in range(HEADS_PER_TILE):
            qb_ref[c] = _rms_scale(acc_scr[c], qkg_ref[2:3, :]).astype(BF16)
        matmul_tile()

    @pl.when(mid_row & (t == KVB_TILE))
    def _():
        for c in range(B_KV_HEADS):
            kvb_ref[c] = _rms_scale(acc_scr[c], qkg_ref[3:4, :]).astype(BF16)
        for c in range(B_KV_HEADS, 2 * B_KV_HEADS):
            kvb_ref[c] = acc_scr[c].astype(BF16)
        matmul_tile()

    @pl.when(mid_row & (t >= GATE_TILE0))
    def _():
        write_gates()
        matmul_tile()


def _in_projection(x2, g1, w_tiles, qk_gains, bsz, seq):
    m = bsz * seq
    tm = TM_IN
    tiles_per_seq = seq // tm
    n_rows = m // tm
    n_steps = n_rows * N_TILES_IN + 1

    def row_of(s):
        return jnp.maximum(s - 1, 0) // N_TILES_IN

    def tile_of(s):
        return jnp.maximum(s - 1, 0) % N_TILES_IN

    def role_of(t):
        return jnp.clip(t // TILES_PER_ROLE, 0, 2)

    def a_map(g):
        def hb(t):
            return jnp.clip(t - TILES_PER_ROLE * role_of(t) - TILES_PER_GROUP * g, 0, TILES_PER_GROUP - 1)
        if g == 0:
            return lambda s: (role_of(tile_of(s)), hb(tile_of(s)), row_of(s), 0)
        return lambda s: (role_of(tile_of(s)), hb(tile_of(s)), row_of(s) // tiles_per_seq, 0,
                          row_of(s) % tiles_per_seq, 0)

    out_shapes = [
        jax.ShapeDtypeStruct((3, A_HPG, m, HEAD_DIM), BF16),
        jax.ShapeDtypeStruct((3, A_HPG, bsz, 4, seq // 4, HEAD_DIM), BF16),
        jax.ShapeDtypeStruct((3, A_HPG, bsz, 16, seq // 16, HEAD_DIM), BF16),
        jax.ShapeDtypeStruct((B_Q_HEADS, m, HEAD_DIM), BF16),
        jax.ShapeDtypeStruct((2 * B_KV_HEADS, m, HEAD_DIM), BF16),
        jax.ShapeDtypeStruct((GATE_TILES, m, TN_IN), BF16),
    ]
    out_specs = [
        pl.BlockSpec((None, HEADS_PER_TILE, tm, HEAD_DIM), a_map(0)),
        pl.BlockSpec((None, HEADS_PER_TILE, None, 4, tm // 4, HEAD_DIM), a_map(1)),
        pl.BlockSpec((None, HEADS_PER_TILE, None, 16, tm // 16, HEAD_DIM), a_map(2)),
        pl.BlockSpec((HEADS_PER_TILE, tm, HEAD_DIM),
                     lambda s: (jnp.clip(tile_of(s) - QB_TILE0, 0, QB_TILES - 1), row_of(s), 0)),
        pl.BlockSpec((2 * B_KV_HEADS, tm, HEAD_DIM), lambda s: (0, row_of(s), 0)),
        pl.BlockSpec((None, tm, TN_IN),
                     lambda s: (jnp.clip(tile_of(s) - GATE_TILE0, 0, GATE_TILES - 1), row_of(s), 0)),
    ]
    in_specs = [
        pl.BlockSpec((tm, D_MODEL), lambda s: (jnp.minimum(s // N_TILES_IN, n_rows - 1), 0)),
        pl.BlockSpec((1, D_MODEL), lambda s: (0, 0)),
        pl.BlockSpec((None, D_MODEL, TN_IN),
                     lambda s: (jnp.where(s < n_steps - 1, s % N_TILES_IN, N_TILES_IN - 1), 0, 0)),
        pl.BlockSpec((4, HEAD_DIM), lambda s: (0, 0)),
    ]
    return pl.pallas_call(
        functools.partial(_inproj_kernel, n_steps=n_steps),
        grid=(n_steps,),
        in_specs=in_specs,
        out_specs=out_specs,
        out_shape=out_shapes,
        scratch_shapes=[pltpu.VMEM((tm, D_MODEL), BF16),
                        pltpu.VMEM((HEADS_PER_TILE, tm, HEAD_DIM), F32)],
        compiler_params=pltpu.CompilerParams(
            dimension_semantics=("arbitrary",), vmem_limit_bytes=VMEM_LIMIT),
        name="in_projection",
    )(x2, g1, w_tiles, qk_gains)


def _band_bias(slope_times_stride, max_back):
    qi = lax.broadcasted_iota(jnp.int32, (BLOCK, 2 * BLOCK), 0) + BLOCK
    ki = lax.broadcasted_iota(jnp.int32, (BLOCK, 2 * BLOCK), 1)
    rel = qi - ki
    valid = (rel >= 0) & (rel <= max_back)
    bias = jnp.where(valid, -slope_times_stride * rel.astype(F32), NEG_INF)
    return bias, ki


def _with_ones(vw):
    return jnp.concatenate([vw, jnp.ones(vw.shape, vw.dtype)], axis=-1)


def _attn_units(units):
    scores = [_dot_nt(q, kw) + bias for q, kw, _, bias in units]
    maxes = [jnp.max(s, axis=-1, keepdims=True) for s in scores]
    probs = [jnp.exp2(s - m).astype(BF16) for s, m in zip(scores, maxes)]
    outs = [_dot(p, _with_ones(u[2])) for p, u in zip(probs, units)]
    results = []
    for o_l, m in zip(outs, maxes):
        o, l = o_l[:, :HEAD_DIM], o_l[:, HEAD_DIM:]
        results.append((o / l, m + jnp.log2(l)))
    return results


UNITS_PER_BATCH = 8


def _mixer_a_kernel(slopes_ref,
                    q0_ref, k0c_ref, k0p_ref, v0c_ref, v0p_ref,
                    q1_ref, k1c_ref, k1p_ref, v1c_ref, v1p_ref,
                    q2_ref, k2c_ref, k2p_ref, v2c_ref, v2p_ref,
                    o_ref, o1_scr, l1_scr, o2_scr, l2_scr):
    first = pl.program_id(1) == 0
    h = pl.program_id(2)

    biases = []
    for g, d in enumerate(DILATIONS):
        bias, ki = _band_bias(slopes_ref[g * A_HPG + h] * float(d), A_MAX_BACK)
        bias_head = jnp.where(first, jnp.where(ki < BLOCK, NEG_INF, bias), bias)
        biases.append((bias, bias_head))

    def window(cur_ref, prev_ref, r, blk):
        if blk == 0:
            cur = cur_ref[0:BLOCK] if r is None else cur_ref[r, 0:BLOCK]
            prev = prev_ref[...] if r is None else prev_ref[r]
            return jnp.concatenate([prev, cur], axis=0)
        lo = (blk - 1) * BLOCK
        return cur_ref[lo:lo + 2 * BLOCK] if r is None else cur_ref[r, lo:lo + 2 * BLOCK]

    def body2(it, carry):
        rs = [it * UNITS_PER_BATCH + u for u in range(UNITS_PER_BATCH)]
        units = [(q2_ref[r], window(k2c_ref, k2p_ref, r, 0), window(v2c_ref, v2p_ref, r, 0), biases[2][1])
                 for r in rs]
        for r, (o, lse) in zip(rs, _attn_units(units)):
            o2_scr[pl.ds(r, BLOCK, stride=16), :] = o
            l2_scr[pl.ds(r, BLOCK, stride=16), :] = lse
        return carry

    lax.fori_loop(0, 16 // UNITS_PER_BATCH, body2, 0)

    def body1(it, carry):
        per = UNITS_PER_BATCH // 4
        rb = [(it * per + u // 4, u % 4) for u in range(UNITS_PER_BATCH)]
        units = [(q1_ref[r, blk * BLOCK:(blk + 1) * BLOCK], window(k1c_ref, k1p_ref, r, blk),
                  window(v1c_ref, v1p_ref, r, blk), biases[1][1] if blk == 0 else biases[1][0])
                 for r, blk in rb]
        for (r, blk), (o, lse) in zip(rb, _attn_units(units)):
            rows = pl.ds(blk * BLOCK * 4 + r, BLOCK, stride=4)
            o1_scr[rows, :] = o
            l1_scr[rows, :] = lse
        return carry

    lax.fori_loop(0, 16 // UNITS_PER_BATCH, body1, 0)

    for b0 in range(0, SUPER // BLOCK, UNITS_PER_BATCH):
        blks = list(range(b0, b0 + UNITS_PER_BATCH))
        units = [(q0_ref[blk * BLOCK:(blk + 1) * BLOCK, :], window(k0c_ref, k0p_ref, None, blk),
                  window(v0c_ref, v0p_ref, None, blk), biases[0][1] if blk == 0 else biases[0][0])
                 for blk in blks]
        for blk, (o0, lse0) in zip(blks, _attn_units(units)):
            rows = slice(blk * BLOCK, (blk + 1) * BLOCK)
            lse1 = l1_scr[rows, :]
            lse2 = l2_scr[rows, :]
            mx = jnp.maximum(jnp.maximum(lse0, lse1), lse2)
            w0 = jnp.exp2(lse0 - mx)
            w1 = jnp.exp2(lse1 - mx)
            w2 = jnp.exp2(lse2 - mx)
            o = (w0 * o0 + w1 * o1_scr[rows, :] + w2 * o2_scr[rows, :]) / (w0 + w1 + w2)
            o_ref[rows, :] = o.astype(BF16)


def _mixer_a(a0, a1, a2, slopes_a, bsz, seq):
    m = bsz * seq
    nsb = seq // SUPER
    grid = (bsz, nsb, A_HPG)

    def nat(role):
        return pl.BlockSpec((None, None, SUPER, HEAD_DIM), lambda b, s, h: (role, h, b * nsb + s, 0))

    def nat_prev(role):
        per = SUPER // BLOCK
        return pl.BlockSpec((None, None, BLOCK, HEAD_DIM),
                            lambda b, s, h: (role, h, jnp.maximum((b * nsb + s) * per - 1, 0), 0))

    def dil(role, d):
        return pl.BlockSpec((None, None, None, d, SUPER // d, HEAD_DIM),
                            lambda b, s, h: (role, h, b, 0, s, 0))

    def dil_prev(role, d):
        per = SUPER // d // BLOCK
        return pl.BlockSpec((None, None, None, d, BLOCK, HEAD_DIM),
                            lambda b, s, h: (role, h, b, 0, jnp.maximum(s * per - 1, 0), 0))

    in_specs = [pl.BlockSpec(memory_space=pltpu.SMEM),
                nat(0), nat(1), nat_prev(1), nat(2), nat_prev(2),
                dil(0, 4), dil(1, 4), dil_prev(1, 4), dil(2, 4), dil_prev(2, 4),
                dil(0, 16), dil(1, 16), dil_prev(1, 16), dil(2, 16), dil_prev(2, 16)]
    return pl.pallas_call(
        _mixer_a_kernel,
        grid=grid,
        in_specs=in_specs,
        out_specs=pl.BlockSpec((None, SUPER, HEAD_DIM), lambda b, s, h: (h, b * nsb + s, 0)),
        out_shape=jax.ShapeDtypeStruct((A_HPG, m, HEAD_DIM), BF16),
        scratch_shapes=[pltpu.VMEM((SUPER, HEAD_DIM), F32) for _ in range(4)],
        compiler_params=pltpu.CompilerParams(
            dimension_semantics=("arbitrary", "arbitrary", "arbitrary"), vmem_limit_bytes=VMEM_LIMIT),
        name="mixer_a",
    )(slopes_a, a0, a0, a0, a0, a0, a1, a1, a1, a1, a1, a2, a2, a2, a2, a2)


def _mixer_b_kernel(slopes_ref, sinks_ref, q_ref, kc_ref, kp_ref, vc_ref, vp_ref, o_ref, *, tiles_per_seq):
    first = (pl.program_id(0) % tiles_per_seq) == 0
    kvh = pl.program_id(1)
    tq = kc_ref.shape[0]

    biases, biases_head = [], []
    for hh in range(B_GROUP):
        bias, ki = _band_bias(slopes_ref[kvh * B_GROUP + hh], B_MAX_BACK)
        biases.append(bias)
        biases_head.append(jnp.where(first, jnp.where(ki < BLOCK, NEG_INF, bias), bias))

    for blk in range(tq // BLOCK):
        if blk == 0:
            kw = jnp.concatenate([kp_ref[...], kc_ref[0:BLOCK]], axis=0)
            vw = jnp.concatenate([vp_ref[...], vc_ref[0:BLOCK]], axis=0)
        else:
            kw = kc_ref[(blk - 1) * BLOCK:(blk + 1) * BLOCK]
            vw = vc_ref[(blk - 1) * BLOCK:(blk + 1) * BLOCK]
        q = q_ref[:, blk * BLOCK:(blk + 1) * BLOCK, :].reshape(B_GROUP * BLOCK, HEAD_DIM)
        s_all = _dot_nt(q, kw)
        ps, sink_terms = [], []
        for hh in range(B_GROUP):
            s = s_all[hh * BLOCK:(hh + 1) * BLOCK] + (biases_head[hh] if blk == 0 else biases[hh])
            sink = sinks_ref[kvh * B_GROUP + hh]
            mrow = jnp.maximum(jnp.max(s, axis=-1, keepdims=True), sink)
            ps.append(jnp.exp2(s - mrow).astype(BF16))
            sink_terms.append(jnp.exp2(sink - mrow))
        o_all = _dot(jnp.concatenate(ps, axis=0), _with_ones(vw))
        for hh in range(B_GROUP):
            rows = slice(hh * BLOCK, (hh + 1) * BLOCK)
            o = o_all[rows, :HEAD_DIM] / (o_all[rows, HEAD_DIM:] + sink_terms[hh])
            o_ref[hh, blk * BLOCK:(blk + 1) * BLOCK, :] = o.astype(BF16)


def _mixer_b(qb, kvb, slopes_b, sinks, bsz, seq):
    m = bsz * seq
    tq = TQ_B
    per = tq // BLOCK
    grid = (m // tq, B_KV_HEADS)

    def cur(off):
        return pl.BlockSpec((None, tq, HEAD_DIM), lambda i, c: (c + off, i, 0))

    def prev(off):
        return pl.BlockSpec((None, BLOCK, HEAD_DIM), lambda i, c: (c + off, jnp.maximum(i * per - 1, 0), 0))

    smem = pl.BlockSpec(memory_space=pltpu.SMEM)
    return pl.pallas_call(
        functools.partial(_mixer_b_kernel, tiles_per_seq=seq // tq),
        grid=grid,
        in_specs=[smem, smem,
                  pl.BlockSpec((B_GROUP, tq, HEAD_DIM), lambda i, c: (c, i, 0)),
                  cur(0), prev(0), cur(B_KV_HEADS), prev(B_KV_HEADS)],
        out_specs=pl.BlockSpec((B_GROUP, tq, HEAD_DIM), lambda i, c: (c, i, 0)),
        out_shape=jax.ShapeDtypeStruct((B_Q_HEADS, m, HEAD_DIM), BF16),
        compiler_params=pltpu.CompilerParams(
            dimension_semantics=("arbitrary", "arbitrary"), vmem_limit_bytes=VMEM_LIMIT),
        name="mixer_b",
    )(slopes_b, sinks, qb, kvb, kvb, kvb, kvb)


def _merge_kernel(oa_ref, ob_ref, gate_ref, wa_ref, wb_ref, out_ref):
    oa = jnp.concatenate([oa_ref[hh] for hh in range(A_HPG)], axis=-1)
    ob = jnp.concatenate([ob_ref[hh] for hh in range(B_Q_HEADS)], axis=-1)
    half = GATE_TILES // 2
    ga = jnp.concatenate([gate_ref[c] for c in range(half)], axis=-1).astype(F32)
    gb = jnp.concatenate([gate_ref[c] for c in range(half, GATE_TILES)], axis=-1).astype(F32)
    merged = ga * _dot(oa, wa_ref[...]) + gb * _dot(ob, wb_ref[...])
    out_ref[...] = merged.astype(BF16)


def _merge(oa, ob, gates, wa, wb):
    m = gates.shape[1]
    tm = TM_MERGE
    resident = dict(pipeline_mode=pl.Buffered(1))
    return pl.pallas_call(
        _merge_kernel,
        grid=(m // tm,),
        in_specs=[pl.BlockSpec((A_HPG, tm, HEAD_DIM), lambda i: (0, i, 0)),
                  pl.BlockSpec((B_Q_HEADS, tm, HEAD_DIM), lambda i: (0, i, 0)),
                  pl.BlockSpec((GATE_TILES, tm, TN_IN), lambda i: (0, i, 0)),
                  pl.BlockSpec(wa.shape, lambda i: (0, 0), **resident),
                  pl.BlockSpec(wb.shape, lambda i: (0, 0), **resident)],
        out_specs=pl.BlockSpec((tm, D_MODEL), lambda i: (i, 0)),
        out_shape=jax.ShapeDtypeStruct((m, D_MODEL), BF16),
        compiler_params=pltpu.CompilerParams(
            dimension_semantics=("arbitrary",), vmem_limit_bytes=VMEM_LIMIT),
        name="gated_merge",
    )(oa, ob, gates, wa, wb)


def _ffn_kernel(x_ref, mg_ref, wo_ref, g2_ref, w1_ref, w2_ref, out_ref, h2_scr):
    @pl.when(pl.program_id(1) == 0)
    def _():
        x1 = x_ref[...] + _dot(mg_ref[...], wo_ref[...])
        out_ref[...] = x1
        h2_scr[...] = _rms_scale(x1, g2_ref[...]).astype(BF16)

    z = _dot(h2_scr[...], w1_ref[...])
    act = jnp.square(jnp.maximum(z, 0.0)).astype(BF16)
    out_ref[...] += _dot(act, w2_ref[...])


def _ffn(x2, merged, wo, g2, w1_tiles, w2):
    m = x2.shape[0]
    tm, tf = TM_FFN, TF_FFN
    return pl.pallas_call(
        _ffn_kernel,
        grid=(m // tm, D_FF // tf),
        in_specs=[pl.BlockSpec((tm, D_MODEL), lambda i, f: (i, 0)),
                  pl.BlockSpec((tm, D_MODEL), lambda i, f: (i, 0)),
                  pl.BlockSpec(wo.shape, lambda i, f: (0, 0), pipeline_mode=pl.Buffered(1)),
                  pl.BlockSpec((1, D_MODEL), lambda i, f: (0, 0)),
                  pl.BlockSpec((None, D_MODEL, tf), lambda i, f: (f, 0, 0)),
                  pl.BlockSpec((tf, D_MODEL), lambda i, f: (f, 0))],
        out_specs=pl.BlockSpec((tm, D_MODEL), lambda i, f: (i, 0)),
        out_shape=jax.ShapeDtypeStruct((m, D_MODEL), F32),
        scratch_shapes=[pltpu.VMEM((tm, D_MODEL), BF16)],
        compiler_params=pltpu.CompilerParams(
            dimension_semantics=("arbitrary", "arbitrary"), vmem_limit_bytes=VMEM_LIMIT),
        name="out_proj_ffn",
    )(x2, merged, wo, g2, w1_tiles, w2)


def _column_tiles(w, tn):
    k, n = w.shape
    return w.reshape(k, n // tn, tn).transpose(1, 0, 2).astype(BF16)


def kernel(x, norm1_g, w_in, q_norm_a, k_norm_a, q_norm_b, k_norm_b, sinks_b, w_branch_a, w_branch_b,
           w_out, norm2_g, w_ff1, w_ff2):
    bsz, seq, d_model = x.shape
    depth = w_in.shape[0]
    assert d_model == D_MODEL and w_in.shape[2] == IN_COLS
    assert seq % SUPER == 0 and seq % TM_IN == 0 and seq % TQ_B == 0
    m = bsz * seq
    slopes = _alibi_slopes() * np.float32(LOG2E)
    slopes_b = jnp.asarray(slopes[:B_Q_HEADS])
    slopes_a = jnp.asarray(slopes[B_Q_HEADS:])
    scale = HEAD_DIM ** -0.5 * LOG2E

    x2 = x.reshape(m, D_MODEL)
    for l in range(depth):
        qk_gains = jnp.stack([q_norm_a[l] * scale, k_norm_a[l], q_norm_b[l] * scale, k_norm_b[l]])
        a0, a1, a2, qb, kvb, gates = _in_projection(
            x2, norm1_g[l].reshape(1, D_MODEL), _column_tiles(w_in[l], TN_IN), qk_gains, bsz, seq)
        oa = _mixer_a(a0, a1, a2, slopes_a, bsz, seq)
        ob = _mixer_b(qb, kvb, slopes_b, sinks_b[l] * LOG2E, bsz, seq)
        merged = _merge(oa, ob, gates, w_branch_a[l].astype(BF16), w_branch_b[l].astype(BF16))
        x2 = _ffn(x2, merged, w_out[l].astype(BF16), norm2_g[l].reshape(1, D_MODEL),
                  _column_tiles(w_ff1[l], TF_FFN), w_ff2[l].astype(BF16))
    return x2.reshape(bsz, seq, D_MODEL)
```

```python
import functools

import numpy as np
import jax
import jax.numpy as jnp
from jax import lax
from jax.experimental import pallas as pl
from jax.experimental.pallas import tpu as pltpu

D_MODEL = 2048
HEAD_DIM = 128
BLOCK = 128
EPS = 1e-6
DILATIONS = (1, 4, 16)
A_MAX_BACK = 128
N_GROUPS = 3
A_HPG = 8
A_WIDTH = N_GROUPS * A_HPG * HEAD_DIM
B_Q_HEADS = 16
B_KV_HEADS = 2
B_GROUP = B_Q_HEADS // B_KV_HEADS
B_MAX_BACK = 127
D_FF = 4 * D_MODEL
N_ATTN_HEADS = B_Q_HEADS + N_GROUPS * A_HPG
IN_COLS = 3 * A_WIDTH + B_Q_HEADS * HEAD_DIM + 2 * B_KV_HEADS * HEAD_DIM + 2 * D_MODEL

SUPER = BLOCK * DILATIONS[-1]
NEG_INF = float("-inf")
LOG2E = 1.4426950408889634
BF16 = jnp.bfloat16
F32 = jnp.float32

VMEM_LIMIT = 56 * 1024 * 1024

TM_IN = 1024
TN_IN = 1024
HEADS_PER_TILE = TN_IN // HEAD_DIM
A_TILES = 3 * N_GROUPS
QB_TILE0 = A_TILES
QB_TILES = B_Q_HEADS // HEADS_PER_TILE
KVB_TILE = QB_TILE0 + QB_TILES
N_TILES_IN = -(-IN_COLS // TN_IN)
N_HEAD_TILES = KVB_TILE + 1
N_HEADS_OUT = N_HEAD_TILES * HEADS_PER_TILE
QB_HEAD0 = QB_TILE0 * HEADS_PER_TILE
KB_HEAD0 = KVB_TILE * HEADS_PER_TILE
VB_HEAD0 = KB_HEAD0 + B_KV_HEADS
GATE_W = TN_IN // 2
GATE_UNITS = 2 * (N_TILES_IN - KVB_TILE)
GATE_UNIT0 = 1

TQ_B = 1024
TM_MERGE = 512
TM_FFN = 512
TF_FFN = 1024


def _alibi_slopes():
    i = np.arange(1, N_ATTN_HEADS + 1, dtype=np.float32)
    return (2.0 ** (-8.0 * i / N_ATTN_HEADS)).astype(np.float32)


def _rms_scale(a, gain):
    ms = jnp.mean(a * a, axis=-1, keepdims=True)
    return a * lax.rsqrt(ms + EPS) * gain


def _dot(a, b):
    return jnp.dot(a, b, preferred_element_type=F32)


def _dot_nt(a, b):
    return lax.dot_general(a, b, (((1,), (1,)), ((), ())), preferred_element_type=F32)


def _inproj_kernel(x_ref, g1_ref, w_ref, qkg_ref, heads_ref, gate_ref, h_scr, acc_scr, tmp_scr, *, n_steps):
    s = pl.program_id(0)
    j = s % N_TILES_IN
    t = jnp.maximum(s - 1, 0) % N_TILES_IN
    has_matmul = s < n_steps - 1
    tm = x_ref.shape[0]

    def matmul_tile():
        acc = _dot(h_scr[...], w_ref[...])
        for c in range(HEADS_PER_TILE):
            acc_scr[c] = acc[:, c * HEAD_DIM:(c + 1) * HEAD_DIM]

    def write_head(c, d, gain):
        def emit(a, res, rows):
            if gain is not None:
                a = _rms_scale(a, gain)
            heads_ref[c, res * rows:(res + 1) * rows, :] = a.astype(BF16)

        if d == 1:
            emit(acc_scr[c], 0, tm)
        elif d == 4:
            for res in range(4):
                emit(acc_scr[c, pl.ds(res, tm // 4, stride=4), :], res, tm // 4)
        else:
            tmp = tmp_scr.at[c % 2]
            for r4 in range(4):
                tmp[r4 * (tm // 4):(r4 + 1) * (tm // 4), :] = acc_scr[c, pl.ds(r4, tm // 4, stride=4), :]
            for r4 in range(4):
                for q in range(4):
                    emit(tmp[pl.ds(r4 * (tm // 4) + q, tm // 16, stride=4), :], r4 + 4 * q, tm // 16)

    def write_gates(c0, unit):
        for c in range(GATE_W // HEAD_DIM):
            z = acc_scr[c0 + c]
            gate_ref[unit, :, c * HEAD_DIM:(c + 1) * HEAD_DIM] = (0.5 + 0.5 * jnp.tanh(0.5 * z)).astype(BF16)

    def write_gate_tile():
        write_gates(0, 0)
        write_gates(GATE_W // HEAD_DIM, 1)

    @pl.when(has_matmul & (j == 0))
    def _():
        @pl.when(s > 0)
        def _():
            write_gate_tile()
        rows = 256
        for c in range(tm // rows):
            x = x_ref[c * rows:(c + 1) * rows, :]
            h_scr[c * rows:(c + 1) * rows, :] = _rms_scale(x, g1_ref[...]).astype(BF16)
        matmul_tile()

    @pl.when(s == n_steps - 1)
    def _():
        write_gate_tile()

    mid_row = has_matmul & (j > 0)
    role = t // N_GROUPS
    group = t % N_GROUPS
    is_a = mid_row & (t < A_TILES)

    for g, d in enumerate(DILATIONS):
        @pl.when(is_a & (group == g) & (role < 2))
        def _(d=d):
            gain = jnp.where(role == 0, qkg_ref[0:1, :], qkg_ref[1:2, :])
            for c in range(HEADS_PER_TILE):
                write_head(c, d, gain)
            matmul_tile()

        @pl.when(is_a & (group == g) & (role == 2))
        def _(d=d):
            for c in range(HEADS_PER_TILE):
                write_head(c, d, None)
            matmul_tile()

    @pl.when(mid_row & (t >= QB_TILE0) & (t < KVB_TILE))
    def _():
        for c in range(HEADS_PER_TILE):
            write_head(c, 1, qkg_ref[2:3, :])
        matmul_tile()

    @pl.when(mid_row & (t == KVB_TILE))
    def _():
        for c in range(B_KV_HEADS):
            write_head(c, 1, qkg_ref[3:4, :])
        for c in range(B_KV_HEADS, HEADS_PER_TILE):
            write_head(c, 1, None)
        write_gate_tile()
        matmul_tile()

    @pl.when(mid_row & (t > KVB_TILE))
    def _():
        write_gate_tile()
        matmul_tile()


def _in_projection(x2, g1, w_pad, qk_gains):
    m = x2.shape[0]
    tm = TM_IN
    n_rows = m // tm
    n_steps = n_rows * N_TILES_IN + 1

    def row_of(s):
        return jnp.maximum(s - 1, 0) // N_TILES_IN

    def tile_of(s):
        return jnp.maximum(s - 1, 0) % N_TILES_IN

    out_shapes = [
        jax.ShapeDtypeStruct((N_HEADS_OUT, n_rows, tm, HEAD_DIM), BF16),
        jax.ShapeDtypeStruct((GATE_UNITS, m, GATE_W), BF16),
    ]
    out_specs = [
        pl.BlockSpec((HEADS_PER_TILE, None, tm, HEAD_DIM),
                     lambda s: (jnp.minimum(tile_of(s), KVB_TILE), row_of(s), 0, 0)),
        pl.BlockSpec((2, tm, GATE_W),
                     lambda s: (jnp.maximum(tile_of(s) - KVB_TILE, 0), row_of(s), 0)),
    ]
    in_specs = [
        pl.BlockSpec((tm, D_MODEL), lambda s: (jnp.minimum(s // N_TILES_IN, n_rows - 1), 0)),
        pl.BlockSpec((1, D_MODEL), lambda s: (0, 0)),
        pl.BlockSpec((D_MODEL, TN_IN),
                     lambda s: (0, jnp.where(s < n_steps - 1, s % N_TILES_IN, N_TILES_IN - 1))),
        pl.BlockSpec((4, HEAD_DIM), lambda s: (0, 0)),
    ]
    return pl.pallas_call(
        functools.partial(_inproj_kernel, n_steps=n_steps),
        grid=(n_steps,),
        in_specs=in_specs,
        out_specs=out_specs,
        out_shape=out_shapes,
        scratch_shapes=[pltpu.VMEM((tm, D_MODEL), BF16),
                        pltpu.VMEM((HEADS_PER_TILE, tm, HEAD_DIM), F32),
                        pltpu.VMEM((2, tm, HEAD_DIM), F32)],
        compiler_params=pltpu.CompilerParams(
            dimension_semantics=("arbitrary",), vmem_limit_bytes=VMEM_LIMIT),
        name="in_projection",
    )(x2, g1, w_pad, qk_gains)


def _band_bias(slope_times_stride, max_back):
    qi = lax.broadcasted_iota(jnp.int32, (BLOCK, 2 * BLOCK), 0) + BLOCK
    ki = lax.broadcasted_iota(jnp.int32, (BLOCK, 2 * BLOCK), 1)
    rel = qi - ki
    valid = (rel >= 0) & (rel <= max_back)
    bias = jnp.where(valid, -slope_times_stride * rel.astype(F32), NEG_INF)
    return bias, ki


def _with_ones(vw):
    return jnp.concatenate([vw, jnp.ones(vw.shape, vw.dtype)], axis=-1)


def _attn_units(units):
    scores = [_dot_nt(q, kw) + bias for q, kw, _, bias in units]
    maxes = [jnp.max(s, axis=-1, keepdims=True) for s in scores]
    probs = [jnp.exp2(s - m).astype(BF16) for s, m in zip(scores, maxes)]
    outs = [_dot(p, _with_ones(u[2])) for p, u in zip(probs, units)]
    results = []
    for o_l, m in zip(outs, maxes):
        o, l = o_l[:, :HEAD_DIM], o_l[:, HEAD_DIM:]
        results.append((o / l, m + jnp.log2(l)))
    return results


UNITS_PER_BATCH = 8
TILES_PER_SUPER = SUPER // TM_IN


def _mixer_a_kernel(slopes_ref,
                    q0_ref, k0c_ref, k0p_ref, v0c_ref, v0p_ref,
                    q1_ref, k1c_ref, k1p_ref, v1c_ref, v1p_ref,
                    q2_ref, k2c_ref, k2p_ref, v2c_ref, v2p_ref,
                    o_ref, o1_scr, l1_scr, o2_scr, l2_scr):
    first = pl.program_id(1) == 0
    h = pl.program_id(2)

    biases = []
    for g, d in enumerate(DILATIONS):
        bias, ki = _band_bias(slopes_ref[g * A_HPG + h] * float(d), A_MAX_BACK)
        bias_head = jnp.where(first, jnp.where(ki < BLOCK, NEG_INF, bias), bias)
        biases.append((bias, bias_head))

    def cat(*parts):
        return jnp.concatenate(parts, axis=0)

    def block16(ref, r):
        return cat(ref[0, r], ref[1, r])

    def body2(it, carry):
        rs = [it * UNITS_PER_BATCH + u for u in range(UNITS_PER_BATCH)]
        units = [(block16(q2_ref, r), cat(block16(k2p_ref, r), block16(k2c_ref, r)),
                  cat(block16(v2p_ref, r), block16(v2c_ref, r)), biases[2][1]) for r in rs]
        for r, (o, lse) in zip(rs, _attn_units(units)):
            o2_scr[pl.ds(r, BLOCK, stride=16), :] = o
            l2_scr[pl.ds(r, BLOCK, stride=16), :] = lse
        return carry

    lax.fori_loop(0, 16 // UNITS_PER_BATCH, body2, 0)

    def block4(ref, r, blk):
        half = (blk % 2) * BLOCK
        return ref[blk // 2, r, half:half + BLOCK]

    def window4(cur_ref, prev_ref, r, blk):
        prev = prev_ref[r] if blk == 0 else block4(cur_ref, r, blk - 1)
        return cat(prev, block4(cur_ref, r, blk))

    def body1(it, carry):
        per = UNITS_PER_BATCH // 4
        rb = [(it * per + u // 4, u % 4) for u in range(UNITS_PER_BATCH)]
        units = [(block4(q1_ref, r, blk), window4(k1c_ref, k1p_ref, r, blk), window4(v1c_ref, v1p_ref, r, blk),
                  biases[1][1] if blk == 0 else biases[1][0]) for r, blk in rb]
        for (r, blk), (o, lse) in zip(rb, _attn_units(units)):
            rows = pl.ds(blk * BLOCK * 4 + r, BLOCK, stride=4)
            o1_scr[rows, :] = o
            l1_scr[rows, :] = lse
        return carry

    lax.fori_loop(0, 16 // UNITS_PER_BATCH, body1, 0)

    def window1(cur_ref, prev_ref, blk):
        if blk == 0:
            return cat(prev_ref[...], cur_ref[0:BLOCK])
        return cur_ref[(blk - 1) * BLOCK:(blk + 1) * BLOCK]

    for b0 in range(0, SUPER // BLOCK, UNITS_PER_BATCH):
        blks = list(range(b0, b0 + UNITS_PER_BATCH))
        units = [(q0_ref[blk * BLOCK:(blk + 1) * BLOCK, :], window1(k0c_ref, k0p_ref, blk),
                  window1(v0c_ref, v0p_ref, blk), biases[0][1] if blk == 0 else biases[0][0])
                 for blk in blks]
        for blk, (o0, lse0) in zip(blks, _attn_units(units)):
            rows = slice(blk * BLOCK, (blk + 1) * BLOCK)
            lse1 = l1_scr[rows, :]
            lse2 = l2_scr[rows, :]
            mx = jnp.maximum(jnp.maximum(lse0, lse1), lse2)
            w0 = jnp.exp2(lse0 - mx)
            w1 = jnp.exp2(lse1 - mx)
            w2 = jnp.exp2(lse2 - mx)
            o = (w0 * o0 + w1 * o1_scr[rows, :] + w2 * o2_scr[rows, :]) / (w0 + w1 + w2)
            o_ref[rows, :] = o.astype(BF16)


def _mixer_a(heads, slopes_a, bsz, seq):
    m = bsz * seq
    nsb = seq // SUPER
    n_rows = m // TM_IN
    grid = (bsz, nsb, A_HPG)
    nat = heads.reshape(N_HEADS_OUT, m, HEAD_DIM)
    by4 = heads.reshape(N_HEADS_OUT, n_rows, 4, TM_IN // 4, HEAD_DIM)
    by16 = heads.reshape(N_HEADS_OUT, n_rows, 16, TM_IN // 16, HEAD_DIM)

    def head(role, g, h):
        return (role * N_GROUPS + g) * A_HPG + h

    def nat_cur(role):
        return pl.BlockSpec((None, SUPER, HEAD_DIM), lambda b, s, h: (head(role, 0, h), b * nsb + s, 0))

    def nat_prev(role):
        per = SUPER // BLOCK
        return pl.BlockSpec((None, BLOCK, HEAD_DIM),
                            lambda b, s, h: (head(role, 0, h), jnp.maximum((b * nsb + s) * per - 1, 0), 0))

    def dil_cur(role, g, d):
        return pl.BlockSpec((None, TILES_PER_SUPER, d, TM_IN // d, HEAD_DIM),
                            lambda b, s, h: (head(role, g, h), b * nsb + s, 0, 0, 0))

    def prev4(role):
        return pl.BlockSpec((None, None, 4, BLOCK, HEAD_DIM),
                            lambda b, s, h: (head(role, 1, h),
                                             jnp.maximum((b * nsb + s) * TILES_PER_SUPER - 1, 0), 0, 1, 0))

    def prev16(role):
        return pl.BlockSpec((None, TILES_PER_SUPER, 16, TM_IN // 16, HEAD_DIM),
                            lambda b, s, h: (head(role, 2, h), jnp.maximum(b * nsb + s - 1, 0), 0, 0, 0))

    in_specs = [pl.BlockSpec(memory_space=pltpu.SMEM),
                nat_cur(0), nat_cur(1), nat_prev(1), nat_cur(2), nat_prev(2),
                dil_cur(0, 1, 4), dil_cur(1, 1, 4), prev4(1), dil_cur(2, 1, 4), prev4(2),
                dil_cur(0, 2, 16), dil_cur(1, 2, 16), prev16(1), dil_cur(2, 2, 16), prev16(2)]
    return pl.pallas_call(
        _mixer_a_kernel,
        grid=grid,
        in_specs=in_specs,
        out_specs=pl.BlockSpec((None, SUPER, HEAD_DIM), lambda b, s, h: (h, b * nsb + s, 0)),
        out_shape=jax.ShapeDtypeStruct((A_HPG, m, HEAD_DIM), BF16),
        scratch_shapes=[pltpu.VMEM((SUPER, HEAD_DIM), F32) for _ in range(4)],
        compiler_params=pltpu.CompilerParams(
            dimension_semantics=("arbitrary", "arbitrary", "arbitrary"), vmem_limit_bytes=VMEM_LIMIT),
        name="mixer_a",
    )(slopes_a, nat, nat, nat, nat, nat, by4, by4, by4, by4, by4, by16, by16, by16, by16, by16)


def _mixer_b_kernel(slopes_ref, sinks_ref, q_ref, kc_ref, kp_ref, vc_ref, vp_ref, o_ref, *, tiles_per_seq):
    first = (pl.program_id(0) % tiles_per_seq) == 0
    kvh = pl.program_id(1)
    tq = kc_ref.shape[0]

    biases, biases_head = [], []
    for hh in range(B_GROUP):
        bias, ki = _band_bias(slopes_ref[kvh * B_GROUP + hh], B_MAX_BACK)
        biases.append(bias)
        biases_head.append(jnp.where(first, jnp.where(ki < BLOCK, NEG_INF, bias), bias))

    for blk in range(tq // BLOCK):
        if blk == 0:
            kw = jnp.concatenate([kp_ref[...], kc_ref[0:BLOCK]], axis=0)
            vw = jnp.concatenate([vp_ref[...], vc_ref[0:BLOCK]], axis=0)
        else:
            kw = kc_ref[(blk - 1) * BLOCK:(blk + 1) * BLOCK]
            vw = vc_ref[(blk - 1) * BLOCK:(blk + 1) * BLOCK]
        q = q_ref[:, blk * BLOCK:(blk + 1) * BLOCK, :].reshape(B_GROUP * BLOCK, HEAD_DIM)
        s_all = _dot_nt(q, kw)
        ps, sink_terms = [], []
        for hh in range(B_GROUP):
            s = s_all[hh * BLOCK:(hh + 1) * BLOCK] + (biases_head[hh] if blk == 0 else biases[hh])
            sink = sinks_ref[kvh * B_GROUP + hh]
            mrow = jnp.maximum(jnp.max(s, axis=-1, keepdims=True), sink)
            ps.append(jnp.exp2(s - mrow).astype(BF16))
            sink_terms.append(jnp.exp2(sink - mrow))
        o_all = _dot(jnp.concatenate(ps, axis=0), _with_ones(vw))
        for hh in range(B_GROUP):
            rows = slice(hh * BLOCK, (hh + 1) * BLOCK)
            o = o_all[rows, :HEAD_DIM] / (o_all[rows, HEAD_DIM:] + sink_terms[hh])
            o_ref[hh, blk * BLOCK:(blk + 1) * BLOCK, :] = o.astype(BF16)


def _mixer_b(heads, slopes_b, sinks, bsz, seq):
    m = bsz * seq
    tq = TQ_B
    per = tq // BLOCK
    grid = (m // tq, B_KV_HEADS)
    nat = heads.reshape(N_HEADS_OUT, m, HEAD_DIM)

    def cur(head0):
        return pl.BlockSpec((None, tq, HEAD_DIM), lambda i, c: (head0 + c, i, 0))

    def prev(head0):
        return pl.BlockSpec((None, BLOCK, HEAD_DIM), lambda i, c: (head0 + c, jnp.maximum(i * per - 1, 0), 0))

    smem = pl.BlockSpec(memory_space=pltpu.SMEM)
    return pl.pallas_call(
        functools.partial(_mixer_b_kernel, tiles_per_seq=seq // tq),
        grid=grid,
        in_specs=[smem, smem,
                  pl.BlockSpec((B_GROUP, tq, HEAD_DIM), lambda i, c: (QB_HEAD0 // B_GROUP + c, i, 0)),
                  cur(KB_HEAD0), prev(KB_HEAD0), cur(VB_HEAD0), prev(VB_HEAD0)],
        out_specs=pl.BlockSpec((B_GROUP, tq, HEAD_DIM), lambda i, c: (c, i, 0)),
        out_shape=jax.ShapeDtypeStruct((B_Q_HEADS, m, HEAD_DIM), BF16),
        compiler_params=pltpu.CompilerParams(
            dimension_semantics=("arbitrary", "arbitrary"), vmem_limit_bytes=VMEM_LIMIT),
        name="mixer_b",
    )(slopes_b, sinks, nat, nat, nat, nat, nat)


def _merge_kernel(oa_ref, ob_ref, gate_ref, wa_ref, wb_ref, out_ref):
    oa = jnp.concatenate([oa_ref[hh] for hh in range(A_HPG)], axis=-1)
    ob = jnp.concatenate([ob_ref[hh] for hh in range(B_Q_HEADS)], axis=-1)
    per = D_MODEL // GATE_W
    ga = jnp.concatenate([gate_ref[GATE_UNIT0 + c] for c in range(per)], axis=-1).astype(F32)
    gb = jnp.concatenate([gate_ref[GATE_UNIT0 + per + c] for c in range(per)], axis=-1).astype(F32)
    merged = ga * _dot(oa, wa_ref[...]) + gb * _dot(ob, wb_ref[...])
    out_ref[...] = merged.astype(BF16)


def _merge(oa, ob, gates, wa, wb):
    m = gates.shape[1]
    tm = TM_MERGE
    resident = dict(pipeline_mode=pl.Buffered(1))
    return pl.pallas_call(
        _merge_kernel,
        grid=(m // tm,),
        in_specs=[pl.BlockSpec((A_HPG, tm, HEAD_DIM), lambda i: (0, i, 0)),
                  pl.BlockSpec((B_Q_HEADS, tm, HEAD_DIM), lambda i: (0, i, 0)),
                  pl.BlockSpec((GATE_UNITS, tm, GATE_W), lambda i: (0, i, 0)),
                  pl.BlockSpec(wa.shape, lambda i: (0, 0), **resident),
                  pl.BlockSpec(wb.shape, lambda i: (0, 0), **resident)],
        out_specs=pl.BlockSpec((tm, D_MODEL), lambda i: (i, 0)),
        out_shape=jax.ShapeDtypeStruct((m, D_MODEL), BF16),
        compiler_params=pltpu.CompilerParams(
            dimension_semantics=("arbitrary",), vmem_limit_bytes=VMEM_LIMIT),
        name="gated_merge",
    )(oa, ob, gates, wa, wb)


def _ffn_kernel(x_ref, mg_ref, wo_ref, g2_ref, w1_ref, w2_ref, out_ref, h2_scr):
    @pl.when(pl.program_id(1) == 0)
    def _():
        x1 = x_ref[...] + _dot(mg_ref[...], wo_ref[...])
        out_ref[...] = x1
        h2_scr[...] = _rms_scale(x1, g2_ref[...]).astype(BF16)

    z = _dot(h2_scr[...], w1_ref[...])
    act = jnp.square(jnp.maximum(z, 0.0)).astype(BF16)
    out_ref[...] += _dot(act, w2_ref[...])


def _ffn(x2, merged, wo, g2, w1, w2):
    m = x2.shape[0]
    tm, tf = TM_FFN, TF_FFN
    return pl.pallas_call(
        _ffn_kernel,
        grid=(m // tm, D_FF // tf),
        in_specs=[pl.BlockSpec((tm, D_MODEL), lambda i, f: (i, 0)),
                  pl.BlockSpec((tm, D_MODEL), lambda i, f: (i, 0)),
                  pl.BlockSpec(wo.shape, lambda i, f: (0, 0), pipeline_mode=pl.Buffered(1)),
                  pl.BlockSpec((1, D_MODEL), lambda i, f: (0, 0)),
                  pl.BlockSpec((D_MODEL, tf), lambda i, f: (0, f)),
                  pl.BlockSpec((tf, D_MODEL), lambda i, f: (f, 0))],
        out_specs=pl.BlockSpec((tm, D_MODEL), lambda i, f: (i, 0)),
        out_shape=jax.ShapeDtypeStruct((m, D_MODEL), F32),
        scratch_shapes=[pltpu.VMEM((tm, D_MODEL), BF16)],
        compiler_params=pltpu.CompilerParams(
            dimension_semantics=("arbitrary", "arbitrary"), vmem_limit_bytes=VMEM_LIMIT),
        name="out_proj_ffn",
    )(x2, merged, wo, g2, w1, w2)


def kernel(x, norm1_g, w_in, q_norm_a, k_norm_a, q_norm_b, k_norm_b, sinks_b, w_branch_a, w_branch_b,
           w_out, norm2_g, w_ff1, w_ff2):
    bsz, seq, d_model = x.shape
    depth = w_in.shape[0]
    assert d_model == D_MODEL and w_in.shape[2] == IN_COLS
    assert seq % SUPER == 0 and seq % TQ_B == 0 and SUPER % TM_IN == 0
    m = bsz * seq
    slopes = _alibi_slopes() * np.float32(LOG2E)
    slopes_b = jnp.asarray(slopes[:B_Q_HEADS])
    slopes_a = jnp.asarray(slopes[B_Q_HEADS:])
    scale = HEAD_DIM ** -0.5 * LOG2E

    x2 = x.reshape(m, D_MODEL)
    for l in range(depth):
        qk_gains = jnp.stack([q_norm_a[l] * scale, k_norm_a[l], q_norm_b[l] * scale, k_norm_b[l]])
        w_pad = jnp.pad(w_in[l].astype(BF16), ((0, 0), (0, N_TILES_IN * TN_IN - IN_COLS)))
        heads, gates = _in_projection(x2, norm1_g[l].reshape(1, D_MODEL), w_pad, qk_gains)
        oa = _mixer_a(heads, slopes_a, bsz, seq)
        ob = _mixer_b(heads, slopes_b, sinks_b[l] * LOG2E, bsz, seq)
        merged = _merge(oa, ob, gates, w_branch_a[l].astype(BF16), w_branch_b[l].astype(BF16))
        x2 = _ffn(x2, merged, w_out[l].astype(BF16), norm2_g[l].reshape(1, D_MODEL),
                  w_ff1[l].astype(BF16), w_ff2[l].astype(BF16))
    return x2.reshape(bsz, seq, D_MODEL)
```

```python
import functools

import numpy as np
import jax
import jax.numpy as jnp
from jax import lax
from jax.experimental import pallas as pl
from jax.experimental.pallas import tpu as pltpu

D_MODEL = 2048
HEAD_DIM = 128
BLOCK = 128
EPS = 1e-6
DILATIONS = (1, 4, 16)
A_MAX_BACK = 128
N_GROUPS = 3
A_HPG = 8
A_WIDTH = N_GROUPS * A_HPG * HEAD_DIM
B_Q_HEADS = 16
B_KV_HEADS = 2
B_GROUP = B_Q_HEADS // B_KV_HEADS
B_MAX_BACK = 127
D_FF = 4 * D_MODEL
N_ATTN_HEADS = B_Q_HEADS + N_GROUPS * A_HPG
IN_COLS = 3 * A_WIDTH + B_Q_HEADS * HEAD_DIM + 2 * B_KV_HEADS * HEAD_DIM + 2 * D_MODEL

SUPER = BLOCK * DILATIONS[-1]
NEG_INF = float("-inf")
LOG2E = 1.4426950408889634
BF16 = jnp.bfloat16
F32 = jnp.float32

VMEM_LIMIT = 56 * 1024 * 1024

TM_IN = 1024
TN_IN = 1024
HEADS_PER_TILE = TN_IN // HEAD_DIM
A_TILES = 3 * N_GROUPS
QB_TILE0 = A_TILES
QB_TILES = B_Q_HEADS // HEADS_PER_TILE
KVB_TILE = QB_TILE0 + QB_TILES
N_TILES_IN = -(-IN_COLS // TN_IN)
TAIL_TILE = N_TILES_IN - 1
TAIL_COLS = IN_COLS - TAIL_TILE * TN_IN
N_HEAD_TILES = KVB_TILE + 1
N_HEADS_OUT = N_HEAD_TILES * HEADS_PER_TILE
QB_HEAD0 = QB_TILE0 * HEADS_PER_TILE
KB_HEAD0 = KVB_TILE * HEADS_PER_TILE
VB_HEAD0 = KB_HEAD0 + B_KV_HEADS
GATE_W = TN_IN // 2
GATE_UNITS = 2 * (N_TILES_IN - KVB_TILE)
GATE_UNIT0 = 1

TQ_B = 1024
TM_MERGE = 512
TM_FFN = 512
TF_FFN = 1024


def _alibi_slopes():
    i = np.arange(1, N_ATTN_HEADS + 1, dtype=np.float32)
    return (2.0 ** (-8.0 * i / N_ATTN_HEADS)).astype(np.float32)


def _rms_scale(a, gain):
    ms = jnp.mean(a * a, axis=-1, keepdims=True)
    return a * lax.rsqrt(ms + EPS) * gain


def _dot(a, b):
    return jnp.dot(a, b, preferred_element_type=F32)


def _dot_nt(a, b):
    return lax.dot_general(a, b, (((1,), (1,)), ((), ())), preferred_element_type=F32)


def _inproj_kernel(x_ref, g1_ref, w_ref, wtail_ref, qkg_ref, heads_ref, gate_ref, h_scr, acc_scr, tmp_scr, *, n_steps):
    s = pl.program_id(0)
    j = s % N_TILES_IN
    t = jnp.maximum(s - 1, 0) % N_TILES_IN
    has_matmul = s < n_steps - 1
    tm = x_ref.shape[0]

    def matmul_tile():
        acc = _dot(h_scr[...], w_ref[...])
        for c in range(HEADS_PER_TILE):
            acc_scr[c] = acc[:, c * HEAD_DIM:(c + 1) * HEAD_DIM]

    def matmul_tail():
        acc = _dot(h_scr[...], wtail_ref[...])
        for c in range(TAIL_COLS // HEAD_DIM):
            acc_scr[c] = acc[:, c * HEAD_DIM:(c + 1) * HEAD_DIM]

    def write_head(c, d, gain):
        def emit(a, res, rows):
            if gain is not None:
                a = _rms_scale(a, gain)
            heads_ref[c, res * rows:(res + 1) * rows, :] = a.astype(BF16)

        if d == 1:
            emit(acc_scr[c], 0, tm)
        elif d == 4:
            for res in range(4):
                emit(acc_scr[c, pl.ds(res, tm // 4, stride=4), :], res, tm // 4)
        else:
            tmp = tmp_scr.at[c % 2]
            for r4 in range(4):
                tmp[r4 * (tm // 4):(r4 + 1) * (tm // 4), :] = acc_scr[c, pl.ds(r4, tm // 4, stride=4), :]
            for r4 in range(4):
                for q in range(4):
                    emit(tmp[pl.ds(r4 * (tm // 4) + q, tm // 16, stride=4), :], r4 + 4 * q, tm // 16)

    def write_gates(c0, unit):
        for c in range(GATE_W // HEAD_DIM):
            z = acc_scr[c0 + c]
            gate_ref[unit, :, c * HEAD_DIM:(c + 1) * HEAD_DIM] = (0.5 + 0.5 * jnp.tanh(0.5 * z)).astype(BF16)

    def write_gate_tile():
        write_gates(0, 0)
        write_gates(GATE_W // HEAD_DIM, 1)

    def write_gate_tail():
        write_gates(0, 0)
        gate_ref[1] = jnp.zeros(gate_ref.shape[1:], BF16)

    @pl.when(has_matmul & (j == 0))
    def _():
        @pl.when(s > 0)
        def _():
            write_gate_tail()
        rows = 256
        for c in range(tm // rows):
            x = x_ref[c * rows:(c + 1) * rows, :]
            h_scr[c * rows:(c + 1) * rows, :] = _rms_scale(x, g1_ref[...]).astype(BF16)
        matmul_tile()

    @pl.when(s == n_steps - 1)
    def _():
        write_gate_tail()

    mid_row = has_matmul & (j > 0)
    role = t // N_GROUPS
    group = t % N_GROUPS
    is_a = mid_row & (t < A_TILES)

    for g, d in enumerate(DILATIONS):
        @pl.when(is_a & (group == g) & (role < 2))
        def _(d=d):
            gain = jnp.where(role == 0, qkg_ref[0:1, :], qkg_ref[1:2, :])
            for c in range(HEADS_PER_TILE):
                write_head(c, d, gain)
            matmul_tile()

        @pl.when(is_a & (group == g) & (role == 2))
        def _(d=d):
            for c in range(HEADS_PER_TILE):
                write_head(c, d, None)
            matmul_tile()

    @pl.when(mid_row & (t >= QB_TILE0) & (t < KVB_TILE))
    def _():
        for c in range(HEADS_PER_TILE):
            write_head(c, 1, qkg_ref[2:3, :])
        matmul_tile()

    @pl.when(mid_row & (t == KVB_TILE))
    def _():
        for c in range(B_KV_HEADS):
            write_head(c, 1, qkg_ref[3:4, :])
        for c in range(B_KV_HEADS, HEADS_PER_TILE):
            write_head(c, 1, None)
        write_gate_tile()
        matmul_tile()

    @pl.when(mid_row & (t > KVB_TILE) & (j < TAIL_TILE))
    def _():
        write_gate_tile()
        matmul_tile()

    @pl.when(mid_row & (j == TAIL_TILE))
    def _():
        write_gate_tile()
        matmul_tail()


def _in_projection(x2, g1, w, w_tail, qk_gains):
    m = x2.shape[0]
    tm = TM_IN
    n_rows = m // tm
    n_steps = n_rows * N_TILES_IN + 1

    def row_of(s):
        return jnp.maximum(s - 1, 0) // N_TILES_IN

    def tile_of(s):
        return jnp.maximum(s - 1, 0) % N_TILES_IN

    out_shapes = [
        jax.ShapeDtypeStruct((N_HEADS_OUT, n_rows, tm, HEAD_DIM), BF16),
        jax.ShapeDtypeStruct((GATE_UNITS, m, GATE_W), BF16),
    ]
    out_specs = [
        pl.BlockSpec((HEADS_PER_TILE, None, tm, HEAD_DIM),
                     lambda s: (jnp.minimum(tile_of(s), KVB_TILE), row_of(s), 0, 0)),
        pl.BlockSpec((2, tm, GATE_W),
                     lambda s: (jnp.maximum(tile_of(s) - KVB_TILE, 0), row_of(s), 0)),
    ]
    in_specs = [
        pl.BlockSpec((tm, D_MODEL), lambda s: (jnp.minimum(s // N_TILES_IN, n_rows - 1), 0)),
        pl.BlockSpec((1, D_MODEL), lambda s: (0, 0)),
        pl.BlockSpec((D_MODEL, TN_IN),
                     lambda s: (0, jnp.where(s < n_steps - 1, jnp.minimum(s % N_TILES_IN, TAIL_TILE - 1),
                                             TAIL_TILE - 1))),
        pl.BlockSpec((D_MODEL, TAIL_COLS), lambda s: (0, 0), pipeline_mode=pl.Buffered(1)),
        pl.BlockSpec((4, HEAD_DIM), lambda s: (0, 0)),
    ]
    return pl.pallas_call(
        functools.partial(_inproj_kernel, n_steps=n_steps),
        grid=(n_steps,),
        in_specs=in_specs,
        out_specs=out_specs,
        out_shape=out_shapes,
        scratch_shapes=[pltpu.VMEM((tm, D_MODEL), BF16),
                        pltpu.VMEM((HEADS_PER_TILE, tm, HEAD_DIM), F32),
                        pltpu.VMEM((2, tm, HEAD_DIM), F32)],
        compiler_params=pltpu.CompilerParams(
            dimension_semantics=("arbitrary",), vmem_limit_bytes=VMEM_LIMIT),
        name="in_projection",
    )(x2, g1, w, w_tail, qk_gains)


def _band_bias(slope_times_stride, max_back):
    qi = lax.broadcasted_iota(jnp.int32, (BLOCK, 2 * BLOCK), 0) + BLOCK
    ki = lax.broadcasted_iota(jnp.int32, (BLOCK, 2 * BLOCK), 1)
    rel = qi - ki
    valid = (rel >= 0) & (rel <= max_back)
    bias = jnp.where(valid, -slope_times_stride * rel.astype(F32), NEG_INF)
    return bias, ki


def _with_ones(vw):
    return jnp.concatenate([vw, jnp.ones(vw.shape, vw.dtype)], axis=-1)


def _attn_units(units):
    scores = [_dot_nt(q, kw) + bias for q, kw, _, bias in units]
    maxes = [jnp.max(s, axis=-1, keepdims=True) for s in scores]
    probs = [jnp.exp2(s - m).astype(BF16) for s, m in zip(scores, maxes)]
    outs = [_dot(p, _with_ones(u[2])) for p, u in zip(probs, units)]
    results = []
    for o_l, m in zip(outs, maxes):
        o, l = o_l[:, :HEAD_DIM], o_l[:, HEAD_DIM:]
        results.append((o / l, m + jnp.log2(l)))
    return results


UNITS_PER_BATCH = 8
TILES_PER_SUPER = SUPER // TM_IN


def _mixer_a_kernel(slopes_ref,
                    q0_ref, k0c_ref, k0p_ref, v0c_ref, v0p_ref,
                    q1_ref, k1c_ref, k1p_ref, v1c_ref, v1p_ref,
                    q2_ref, k2c_ref, k2p_ref, v2c_ref, v2p_ref,
                    o_ref, o1_scr, l1_scr, o2_scr, l2_scr):
    first = pl.program_id(1) == 0
    h = pl.program_id(2)

    biases = []
    for g, d in enumerate(DILATIONS):
        bias, ki = _band_bias(slopes_ref[g * A_HPG + h] * float(d), A_MAX_BACK)
        bias_head = jnp.where(first, jnp.where(ki < BLOCK, NEG_INF, bias), bias)
        biases.append((bias, bias_head))

    def cat(*parts):
        return jnp.concatenate(parts, axis=0)

    def block16(ref, r):
        return cat(ref[0, r], ref[1, r])

    def body2(it, carry):
        rs = [it * UNITS_PER_BATCH + u for u in range(UNITS_PER_BATCH)]
        units = [(block16(q2_ref, r), cat(block16(k2p_ref, r), block16(k2c_ref, r)),
                  cat(block16(v2p_ref, r), block16(v2c_ref, r)), biases[2][1]) for r in rs]
        for r, (o, lse) in zip(rs, _attn_units(units)):
            o2_scr[pl.ds(r, BLOCK, stride=16), :] = o
            l2_scr[pl.ds(r, BLOCK, stride=16), :] = lse
        return carry

    lax.fori_loop(0, 16 // UNITS_PER_BATCH, body2, 0)

    def block4(ref, r, blk):
        half = (blk % 2) * BLOCK
        return ref[blk // 2, r, half:half + BLOCK]

    def window4(cur_ref, prev_ref, r, blk):
        prev = prev_ref[r] if blk == 0 else block4(cur_ref, r, blk - 1)
        return cat(prev, block4(cur_ref, r, blk))

    def body1(it, carry):
        per = UNITS_PER_BATCH // 4
        rb = [(it * per + u // 4, u % 4) for u in range(UNITS_PER_BATCH)]
        units = [(block4(q1_ref, r, blk), window4(k1c_ref, k1p_ref, r, blk), window4(v1c_ref, v1p_ref, r, blk),
                  biases[1][1] if blk == 0 else biases[1][0]) for r, blk in rb]
        for (r, blk), (o, lse) in zip(rb, _attn_units(units)):
            rows = pl.ds(blk * BLOCK * 4 + r, BLOCK, stride=4)
            o1_scr[rows, :] = o
            l1_scr[rows, :] = lse
        return carry

    lax.fori_loop(0, 16 // UNITS_PER_BATCH, body1, 0)

    def window1(cur_ref, prev_ref, blk):
        if blk == 0:
            return cat(prev_ref[...], cur_ref[0:BLOCK])
        return cur_ref[(blk - 1) * BLOCK:(blk + 1) * BLOCK]

    for b0 in range(0, SUPER // BLOCK, UNITS_PER_BATCH):
        blks = list(range(b0, b0 + UNITS_PER_BATCH))
        units = [(q0_ref[blk * BLOCK:(blk + 1) * BLOCK, :], window1(k0c_ref, k0p_ref, blk),
                  window1(v0c_ref, v0p_ref, blk), biases[0][1] if blk == 0 else biases[0][0])
                 for blk in blks]
        for blk, (o0, lse0) in zip(blks, _attn_units(units)):
            rows = slice(blk * BLOCK, (blk + 1) * BLOCK)
            lse1 = l1_scr[rows, :]
            lse2 = l2_scr[rows, :]
            mx = jnp.maximum(jnp.maximum(lse0, lse1), lse2)
            w0 = jnp.exp2(lse0 - mx)
            w1 = jnp.exp2(lse1 - mx)
            w2 = jnp.exp2(lse2 - mx)
            o = (w0 * o0 + w1 * o1_scr[rows, :] + w2 * o2_scr[rows, :]) / (w0 + w1 + w2)
            o_ref[rows, :] = o.astype(BF16)


def _mixer_a(heads, slopes_a, bsz, seq):
    m = bsz * seq
    nsb = seq // SUPER
    n_rows = m // TM_IN
    grid = (bsz, nsb, A_HPG)
    nat = heads.reshape(N_HEADS_OUT, m, HEAD_DIM)
    by4 = heads.reshape(N_HEADS_OUT, n_rows, 4, TM_IN // 4, HEAD_DIM)
    by16 = heads.reshape(N_HEADS_OUT, n_rows, 16, TM_IN // 16, HEAD_DIM)

    def head(role, g, h):
        return (role * N_GROUPS + g) * A_HPG + h

    def nat_cur(role):
        return pl.BlockSpec((None, SUPER, HEAD_DIM), lambda b, s, h: (head(role, 0, h), b * nsb + s, 0))

    def nat_prev(role):
        per = SUPER // BLOCK
        return pl.BlockSpec((None, BLOCK, HEAD_DIM),
                            lambda b, s, h: (head(role, 0, h), jnp.maximum((b * nsb + s) * per - 1, 0), 0))

    def dil_cur(role, g, d):
        return pl.BlockSpec((None, TILES_PER_SUPER, d, TM_IN // d, HEAD_DIM),
                            lambda b, s, h: (head(role, g, h), b * nsb + s, 0, 0, 0))

    def prev4(role):
        return pl.BlockSpec((None, None, 4, BLOCK, HEAD_DIM),
                            lambda b, s, h: (head(role, 1, h),
                                             jnp.maximum((b * nsb + s) * TILES_PER_SUPER - 1, 0), 0, 1, 0))

    def prev16(role):
        return pl.BlockSpec((None, TILES_PER_SUPER, 16, TM_IN // 16, HEAD_DIM),
                            lambda b, s, h: (head(role, 2, h), jnp.maximum(b * nsb + s - 1, 0), 0, 0, 0))

    in_specs = [pl.BlockSpec(memory_space=pltpu.SMEM),
                nat_cur(0), nat_cur(1), nat_prev(1), nat_cur(2), nat_prev(2),
                dil_cur(0, 1, 4), dil_cur(1, 1, 4), prev4(1), dil_cur(2, 1, 4), prev4(2),
                dil_cur(0, 2, 16), dil_cur(1, 2, 16), prev16(1), dil_cur(2, 2, 16), prev16(2)]
    return pl.pallas_call(
        _mixer_a_kernel,
        grid=grid,
        in_specs=in_specs,
        out_specs=pl.BlockSpec((None, SUPER, HEAD_DIM), lambda b, s, h: (h, b * nsb + s, 0)),
        out_shape=jax.ShapeDtypeStruct((A_HPG, m, HEAD_DIM), BF16),
        scratch_shapes=[pltpu.VMEM((SUPER, HEAD_DIM), F32) for _ in range(4)],
        compiler_params=pltpu.CompilerParams(
            dimension_semantics=("arbitrary", "arbitrary", "arbitrary"), vmem_limit_bytes=VMEM_LIMIT),
        name="mixer_a",
    )(slopes_a, nat, nat, nat, nat, nat, by4, by4, by4, by4, by4, by16, by16, by16, by16, by16)


def _mixer_b_kernel(slopes_ref, sinks_ref, q_ref, kc_ref, kp_ref, vc_ref, vp_ref, o_ref, *, tiles_per_seq):
    first = (pl.program_id(0) % tiles_per_seq) == 0
    kvh = pl.program_id(1)
    tq = kc_ref.shape[0]

    biases, biases_head = [], []
    for hh in range(B_GROUP):
        bias, ki = _band_bias(slopes_ref[kvh * B_GROUP + hh], B_MAX_BACK)
        biases.append(bias)
        biases_head.append(jnp.where(first, jnp.where(ki < BLOCK, NEG_INF, bias), bias))

    for blk in range(tq // BLOCK):
        if blk == 0:
            kw = jnp.concatenate([kp_ref[...], kc_ref[0:BLOCK]], axis=0)
            vw = jnp.concatenate([vp_ref[...], vc_ref[0:BLOCK]], axis=0)
        else:
            kw = kc_ref[(blk - 1) * BLOCK:(blk + 1) * BLOCK]
            vw = vc_ref[(blk - 1) * BLOCK:(blk + 1) * BLOCK]
        q = q_ref[:, blk * BLOCK:(blk + 1) * BLOCK, :].reshape(B_GROUP * BLOCK, HEAD_DIM)
        s_all = _dot_nt(q, kw)
        ps, sink_terms = [], []
        for hh in range(B_GROUP):
            s = s_all[hh * BLOCK:(hh + 1) * BLOCK] + (biases_head[hh] if blk == 0 else biases[hh])
            sink = sinks_ref[kvh * B_GROUP + hh]
            mrow = jnp.maximum(jnp.max(s, axis=-1, keepdims=True), sink)
            ps.append(jnp.exp2(s - mrow).astype(BF16))
            sink_terms.append(jnp.exp2(sink - mrow))
        o_all = _dot(jnp.concatenate(ps, axis=0), _with_ones(vw))
        for hh in range(B_GROUP):
            rows = slice(hh * BLOCK, (hh + 1) * BLOCK)
            o = o_all[rows, :HEAD_DIM] / (o_all[rows, HEAD_DIM:] + sink_terms[hh])
            o_ref[hh, blk * BLOCK:(blk + 1) * BLOCK, :] = o.astype(BF16)


def _mixer_b(heads, slopes_b, sinks, bsz, seq):
    m = bsz * seq
    tq = TQ_B
    per = tq // BLOCK
    grid = (m // tq, B_KV_HEADS)
    nat = heads.reshape(N_HEADS_OUT, m, HEAD_DIM)

    def cur(head0):
        return pl.BlockSpec((None, tq, HEAD_DIM), lambda i, c: (head0 + c, i, 0))

    def prev(head0):
        return pl.BlockSpec((None, BLOCK, HEAD_DIM), lambda i, c: (head0 + c, jnp.maximum(i * per - 1, 0), 0))

    smem = pl.BlockSpec(memory_space=pltpu.SMEM)
    return pl.pallas_call(
        functools.partial(_mixer_b_kernel, tiles_per_seq=seq // tq),
        grid=grid,
        in_specs=[smem, smem,
                  pl.BlockSpec((B_GROUP, tq, HEAD_DIM), lambda i, c: (QB_HEAD0 // B_GROUP + c, i, 0)),
                  cur(KB_HEAD0), prev(KB_HEAD0), cur(VB_HEAD0), prev(VB_HEAD0)],
        out_specs=pl.BlockSpec((B_GROUP, tq, HEAD_DIM), lambda i, c: (c, i, 0)),
        out_shape=jax.ShapeDtypeStruct((B_Q_HEADS, m, HEAD_DIM), BF16),
        compiler_params=pltpu.CompilerParams(
            dimension_semantics=("arbitrary", "arbitrary"), vmem_limit_bytes=VMEM_LIMIT),
        name="mixer_b",
    )(slopes_b, sinks, nat, nat, nat, nat, nat)


def _merge_kernel(oa_ref, ob_ref, gate_ref, wa_ref, wb_ref, out_ref):
    oa = jnp.concatenate([oa_ref[hh] for hh in range(A_HPG)], axis=-1)
    ob = jnp.concatenate([ob_ref[hh] for hh in range(B_Q_HEADS)], axis=-1)
    per = D_MODEL // GATE_W
    ga = jnp.concatenate([gate_ref[GATE_UNIT0 + c] for c in range(per)], axis=-1).astype(F32)
    gb = jnp.concatenate([gate_ref[GATE_UNIT0 + per + c] for c in range(per)], axis=-1).astype(F32)
    merged = ga * _dot(oa, wa_ref[...]) + gb * _dot(ob, wb_ref[...])
    out_ref[...] = merged.astype(BF16)


def _merge(oa, ob, gates, wa, wb):
    m = gates.shape[1]
    tm = TM_MERGE
    resident = dict(pipeline_mode=pl.Buffered(1))
    return pl.pallas_call(
        _merge_kernel,
        grid=(m // tm,),
        in_specs=[pl.BlockSpec((A_HPG, tm, HEAD_DIM), lambda i: (0, i, 0)),
                  pl.BlockSpec((B_Q_HEADS, tm, HEAD_DIM), lambda i: (0, i, 0)),
                  pl.BlockSpec((GATE_UNITS, tm, GATE_W), lambda i: (0, i, 0)),
                  pl.BlockSpec(wa.shape, lambda i: (0, 0), **resident),
                  pl.BlockSpec(wb.shape, lambda i: (0, 0), **resident)],
        out_specs=pl.BlockSpec((tm, D_MODEL), lambda i: (i, 0)),
        out_shape=jax.ShapeDtypeStruct((m, D_MODEL), BF16),
        compiler_params=pltpu.CompilerParams(
            dimension_semantics=("arbitrary",), vmem_limit_bytes=VMEM_LIMIT),
        name="gated_merge",
    )(oa, ob, gates, wa, wb)


def _ffn_kernel(x_ref, mg_ref, wo_ref, g2_ref, w1_ref, w2_ref, out_ref, h2_scr):
    @pl.when(pl.program_id(1) == 0)
    def _():
        x1 = x_ref[...] + _dot(mg_ref[...], wo_ref[...])
        out_ref[...] = x1
        h2_scr[...] = _rms_scale(x1, g2_ref[...]).astype(BF16)

    z = _dot(h2_scr[...], w1_ref[...])
    act = jnp.square(jnp.maximum(z, 0.0)).astype(BF16)
    out_ref[...] += _dot(act, w2_ref[...])


def _ffn(x2, merged, wo, g2, w1, w2):
    m = x2.shape[0]
    tm, tf = TM_FFN, TF_FFN
    return pl.pallas_call(
        _ffn_kernel,
        grid=(m // tm, D_FF // tf),
        in_specs=[pl.BlockSpec((tm, D_MODEL), lambda i, f: (i, 0)),
                  pl.BlockSpec((tm, D_MODEL), lambda i, f: (i, 0)),
                  pl.BlockSpec(wo.shape, lambda i, f: (0, 0), pipeline_mode=pl.Buffered(1)),
                  pl.BlockSpec((1, D_MODEL), lambda i, f: (0, 0)),
                  pl.BlockSpec((D_MODEL, tf), lambda i, f: (0, f)),
                  pl.BlockSpec((tf, D_MODEL), lambda i, f: (f, 0))],
        out_specs=pl.BlockSpec((tm, D_MODEL), lambda i, f: (i, 0)),
        out_shape=jax.ShapeDtypeStruct((m, D_MODEL), F32),
        scratch_shapes=[pltpu.VMEM((tm, D_MODEL), BF16)],
        compiler_params=pltpu.CompilerParams(
            dimension_semantics=("arbitrary", "arbitrary"), vmem_limit_bytes=VMEM_LIMIT),
        name="out_proj_ffn",
    )(x2, merged, wo, g2, w1, w2)


def kernel(x, norm1_g, w_in, q_norm_a, k_norm_a, q_norm_b, k_norm_b, sinks_b, w_branch_a, w_branch_b,
           w_out, norm2_g, w_ff1, w_ff2):
    bsz, seq, d_model = x.shape
    depth = w_in.shape[0]
    assert d_model == D_MODEL and w_in.shape[2] == IN_COLS
    assert seq % SUPER == 0 and seq % TQ_B == 0 and SUPER % TM_IN == 0
    m = bsz * seq
    slopes = _alibi_slopes() * np.float32(LOG2E)
    slopes_b = jnp.asarray(slopes[:B_Q_HEADS])
    slopes_a = jnp.asarray(slopes[B_Q_HEADS:])
    scale = HEAD_DIM ** -0.5 * LOG2E

    x2 = x.reshape(m, D_MODEL)
    for l in range(depth):
        qk_gains = jnp.stack([q_norm_a[l] * scale, k_norm_a[l], q_norm_b[l] * scale, k_norm_b[l]])
        w_tail = w_in[l][:, TAIL_TILE * TN_IN:].astype(BF16)
        heads, gates = _in_projection(x2, norm1_g[l].reshape(1, D_MODEL), w_in[l].astype(BF16), w_tail, qk_gains)
        oa = _mixer_a(heads, slopes_a, bsz, seq)
        ob = _mixer_b(heads, slopes_b, sinks_b[l] * LOG2E, bsz, seq)
        merged = _merge(oa, ob, gates, w_branch_a[l].astype(BF16), w_branch_b[l].astype(BF16))
        x2 = _ffn(x2, merged, w_out[l].astype(BF16), norm2_g[l].reshape(1, D_MODEL),
                  w_ff1[l].astype(BF16), w_ff2[l].astype(BF16))
    return x2.reshape(bsz, seq, D_MODEL)
```

```python
import functools

import numpy as np
import jax
import jax.numpy as jnp
from jax import lax
from jax.experimental import pallas as pl
from jax.experimental.pallas import tpu as pltpu

D_MODEL = 2048
HEAD_DIM = 128
BLOCK = 128
EPS = 1e-6
DILATIONS = (1, 4, 16)
A_MAX_BACK = 128
N_GROUPS = 3
A_HPG = 8
A_WIDTH = N_GROUPS * A_HPG * HEAD_DIM
B_Q_HEADS = 16
B_KV_HEADS = 2
B_GROUP = B_Q_HEADS // B_KV_HEADS
B_MAX_BACK = 127
D_FF = 4 * D_MODEL
N_ATTN_HEADS = B_Q_HEADS + N_GROUPS * A_HPG
IN_COLS = 3 * A_WIDTH + B_Q_HEADS * HEAD_DIM + 2 * B_KV_HEADS * HEAD_DIM + 2 * D_MODEL

SUPER = BLOCK * DILATIONS[-1]
NEG_INF = float("-inf")
LOG2E = 1.4426950408889634
BF16 = jnp.bfloat16
F32 = jnp.float32

VMEM_LIMIT = 56 * 1024 * 1024

TM_IN = 1024
TN_IN = 1024
HEADS_PER_TILE = TN_IN // HEAD_DIM
A_TILES = 3 * N_GROUPS
QB_TILE0 = A_TILES
QB_TILES = B_Q_HEADS // HEADS_PER_TILE
KVB_TILE = QB_TILE0 + QB_TILES
N_TILES_IN = -(-IN_COLS // TN_IN)
TAIL_TILE = N_TILES_IN - 1
TAIL_COLS = IN_COLS - TAIL_TILE * TN_IN
N_HEAD_TILES = KVB_TILE + 1
N_HEADS_OUT = N_HEAD_TILES * HEADS_PER_TILE
QB_HEAD0 = QB_TILE0 * HEADS_PER_TILE
KB_HEAD0 = KVB_TILE * HEADS_PER_TILE
VB_HEAD0 = KB_HEAD0 + B_KV_HEADS
GATE_W = TN_IN // 2
GATE_UNITS = 2 * (N_TILES_IN - KVB_TILE)
GATE_UNIT0 = 1

TQ_B = 1024
TM_MERGE = 512
TM_FFN = 512
TF_FFN = 1024


def _alibi_slopes():
    i = np.arange(1, N_ATTN_HEADS + 1, dtype=np.float32)
    return (2.0 ** (-8.0 * i / N_ATTN_HEADS)).astype(np.float32)


def _rms_scale(a, gain):
    ms = jnp.mean(a * a, axis=-1, keepdims=True)
    return a * lax.rsqrt(ms + EPS) * gain


def _dot(a, b):
    return jnp.dot(a, b, preferred_element_type=F32)


def _dot_nt(a, b):
    return lax.dot_general(a, b, (((1,), (1,)), ((), ())), preferred_element_type=F32)


def _inproj_kernel(x_ref, g1_ref, w_ref, wtail_ref, qkg_ref, heads_ref, gate_ref, h_scr, acc_scr, tmp_scr, *, n_steps):
    s = pl.program_id(0)
    p = s % N_TILES_IN
    t = jnp.maximum(s - 1, 0) % N_TILES_IN - 1
    has_matmul = s < n_steps - 1
    tm = x_ref.shape[0]

    def matmul_tile():
        acc = _dot(h_scr[...], w_ref[...])
        for c in range(HEADS_PER_TILE):
            acc_scr[c] = acc[:, c * HEAD_DIM:(c + 1) * HEAD_DIM]

    def matmul_tail():
        acc = _dot(h_scr[...], wtail_ref[...])
        for c in range(TAIL_COLS // HEAD_DIM):
            acc_scr[c] = acc[:, c * HEAD_DIM:(c + 1) * HEAD_DIM]

    def write_head(c, d, gain):
        def emit(a, res, rows):
            if gain is not None:
                a = _rms_scale(a, gain)
            heads_ref[c, res * rows:(res + 1) * rows, :] = a.astype(BF16)

        if d == 1:
            emit(acc_scr[c], 0, tm)
        elif d == 4:
            for res in range(4):
                emit(acc_scr[c, pl.ds(res, tm // 4, stride=4), :], res, tm // 4)
        else:
            tmp = tmp_scr.at[c % 2]
            for r4 in range(4):
                tmp[r4 * (tm // 4):(r4 + 1) * (tm // 4), :] = acc_scr[c, pl.ds(r4, tm // 4, stride=4), :]
            for r4 in range(4):
                for q in range(4):
                    emit(tmp[pl.ds(r4 * (tm // 4) + q, tm // 16, stride=4), :], r4 + 4 * q, tm // 16)

    def write_gates(c0, unit):
        for c in range(GATE_W // HEAD_DIM):
            z = acc_scr[c0 + c]
            gate_ref[unit, :, c * HEAD_DIM:(c + 1) * HEAD_DIM] = (0.5 + 0.5 * jnp.tanh(0.5 * z)).astype(BF16)

    def write_gate_tile():
        write_gates(0, 0)
        write_gates(GATE_W // HEAD_DIM, 1)

    def write_gate_tail():
        write_gates(0, 0)
        gate_ref[1] = jnp.zeros(gate_ref.shape[1:], BF16)

    @pl.when(has_matmul & (p == 0))
    def _():
        @pl.when(s > 0)
        def _():
            write_gate_tile()
        rows = 256
        for c in range(tm // rows):
            x = x_ref[c * rows:(c + 1) * rows, :]
            h_scr[c * rows:(c + 1) * rows, :] = _rms_scale(x, g1_ref[...]).astype(BF16)
        matmul_tail()

    @pl.when(s == n_steps - 1)
    def _():
        write_gate_tile()

    @pl.when(has_matmul & (p == 1))
    def _():
        write_gate_tail()
        matmul_tile()

    mid_row = has_matmul & (p > 1)
    role = t // N_GROUPS
    group = t % N_GROUPS
    is_a = mid_row & (t < A_TILES)

    for g, d in enumerate(DILATIONS):
        @pl.when(is_a & (group == g) & (role < 2))
        def _(d=d):
            gain = jnp.where(role == 0, qkg_ref[0:1, :], qkg_ref[1:2, :])
            for c in range(HEADS_PER_TILE):
                write_head(c, d, gain)
            matmul_tile()

        @pl.when(is_a & (group == g) & (role == 2))
        def _(d=d):
            for c in range(HEADS_PER_TILE):
                write_head(c, d, None)
            matmul_tile()

    @pl.when(mid_row & (t >= QB_TILE0) & (t < KVB_TILE))
    def _():
        for c in range(HEADS_PER_TILE):
            write_head(c, 1, qkg_ref[2:3, :])
        matmul_tile()

    @pl.when(mid_row & (t == KVB_TILE))
    def _():
        for c in range(B_KV_HEADS):
            write_head(c, 1, qkg_ref[3:4, :])
        for c in range(B_KV_HEADS, HEADS_PER_TILE):
            write_head(c, 1, None)
        write_gate_tile()
        matmul_tile()

    @pl.when(mid_row & (t > KVB_TILE))
    def _():
        write_gate_tile()
        matmul_tile()


def _in_projection(x2, g1, w, w_tail, qk_gains):
    m = x2.shape[0]
    tm = TM_IN
    n_rows = m // tm
    n_steps = n_rows * N_TILES_IN + 1

    def row_of(s):
        return jnp.maximum(s - 1, 0) // N_TILES_IN

    def tile_of(s):
        return jnp.maximum(s - 1, 0) % N_TILES_IN - 1

    out_shapes = [
        jax.ShapeDtypeStruct((N_HEADS_OUT, n_rows, tm, HEAD_DIM), BF16),
        jax.ShapeDtypeStruct((GATE_UNITS, m, GATE_W), BF16),
    ]
    out_specs = [
        pl.BlockSpec((HEADS_PER_TILE, None, tm, HEAD_DIM),
                     lambda s: (jnp.clip(tile_of(s), 0, KVB_TILE), row_of(s), 0, 0)),
        pl.BlockSpec((2, tm, GATE_W),
                     lambda s: (jnp.where(tile_of(s) >= KVB_TILE, tile_of(s) - KVB_TILE, TAIL_TILE - KVB_TILE),
                                row_of(s), 0)),
    ]
    in_specs = [
        pl.BlockSpec((tm, D_MODEL), lambda s: (jnp.minimum(s // N_TILES_IN, n_rows - 1), 0)),
        pl.BlockSpec((1, D_MODEL), lambda s: (0, 0)),
        pl.BlockSpec((D_MODEL, TN_IN),
                     lambda s: (0, jnp.where(s < n_steps - 1, jnp.maximum(s % N_TILES_IN - 1, 0),
                                             TAIL_TILE - 1))),
        pl.BlockSpec((D_MODEL, TAIL_COLS), lambda s: (0, 0), pipeline_mode=pl.Buffered(1)),
        pl.BlockSpec((4, HEAD_DIM), lambda s: (0, 0)),
    ]
    return pl.pallas_call(
        functools.partial(_inproj_kernel, n_steps=n_steps),
        grid=(n_steps,),
        in_specs=in_specs,
        out_specs=out_specs,
        out_shape=out_shapes,
        scratch_shapes=[pltpu.VMEM((tm, D_MODEL), BF16),
                        pltpu.VMEM((HEADS_PER_TILE, tm, HEAD_DIM), F32),
                        pltpu.VMEM((2, tm, HEAD_DIM), F32)],
        compiler_params=pltpu.CompilerParams(
            dimension_semantics=("arbitrary",), vmem_limit_bytes=VMEM_LIMIT),
        name="in_projection",
    )(x2, g1, w, w_tail, qk_gains)


def _band_bias(slope_times_stride, max_back):
    qi = lax.broadcasted_iota(jnp.int32, (BLOCK, 2 * BLOCK), 0) + BLOCK
    ki = lax.broadcasted_iota(jnp.int32, (BLOCK, 2 * BLOCK), 1)
    rel = qi - ki
    valid = (rel >= 0) & (rel <= max_back)
    bias = jnp.where(valid, -slope_times_stride * rel.astype(F32), NEG_INF)
    return bias, ki


def _with_ones(vw):
    return jnp.concatenate([vw, jnp.ones(vw.shape, vw.dtype)], axis=-1)


def _attn_units(units):
    scores = [_dot_nt(q, kw) + bias for q, kw, _, bias in units]
    maxes = [jnp.max(s, axis=-1, keepdims=True) for s in scores]
    probs = [jnp.exp2(s - m).astype(BF16) for s, m in zip(scores, maxes)]
    outs = [_dot(p, _with_ones(u[2])) for p, u in zip(probs, units)]
    results = []
    for o_l, m in zip(outs, maxes):
        o, l = o_l[:, :HEAD_DIM], o_l[:, HEAD_DIM:]
        results.append((o / l, m + jnp.log2(l)))
    return results


UNITS_PER_BATCH = 16
TILES_PER_SUPER = SUPER // TM_IN


def _mixer_a_kernel(slopes_ref,
                    q0_ref, k0c_ref, k0p_ref, v0c_ref, v0p_ref,
                    q1_ref, k1c_ref, k1p_ref, v1c_ref, v1p_ref,
                    q2_ref, k2c_ref, k2p_ref, v2c_ref, v2p_ref, w1_ref, w2_ref,
                    o_ref, w1_out_ref, w2_out_ref, o1_scr, l1_scr, o2_scr, l2_scr):
    w1_out_ref[...] = w1_ref[...].astype(BF16)
    w2_out_ref[...] = w2_ref[...].astype(BF16)

    first = pl.program_id(1) == 0
    h = pl.program_id(2)

    biases = []
    for g, d in enumerate(DILATIONS):
        bias, ki = _band_bias(slopes_ref[g * A_HPG + h] * float(d), A_MAX_BACK)
        bias_head = jnp.where(first, jnp.where(ki < BLOCK, NEG_INF, bias), bias)
        biases.append((bias, bias_head))

    def cat(*parts):
        return jnp.concatenate(parts, axis=0)

    def block16(ref, r):
        return cat(ref[0, r], ref[1, r])

    def body2(it, carry):
        rs = [it * UNITS_PER_BATCH + u for u in range(UNITS_PER_BATCH)]
        units = [(block16(q2_ref, r), cat(block16(k2p_ref, r), block16(k2c_ref, r)),
                  cat(block16(v2p_ref, r), block16(v2c_ref, r)), biases[2][1]) for r in rs]
        for r, (o, lse) in zip(rs, _attn_units(units)):
            o2_scr[pl.ds(r, BLOCK, stride=16), :] = o
            l2_scr[pl.ds(r, BLOCK, stride=16), :] = lse
        return carry

    lax.fori_loop(0, 16 // UNITS_PER_BATCH, body2, 0)

    def block4(ref, r, blk):
        half = (blk % 2) * BLOCK
        return ref[blk // 2, r, half:half + BLOCK]

    def window4(cur_ref, prev_ref, r, blk):
        prev = prev_ref[r] if blk == 0 else block4(cur_ref, r, blk - 1)
        return cat(prev, block4(cur_ref, r, blk))

    def body1(it, carry):
        per = UNITS_PER_BATCH // 4
        rb = [(it * per + u // 4, u % 4) for u in range(UNITS_PER_BATCH)]
        units = [(block4(q1_ref, r, blk), window4(k1c_ref, k1p_ref, r, blk), window4(v1c_ref, v1p_ref, r, blk),
                  biases[1][1] if blk == 0 else biases[1][0]) for r, blk in rb]
        for (r, blk), (o, lse) in zip(rb, _attn_units(units)):
            rows = pl.ds(blk * BLOCK * 4 + r, BLOCK, stride=4)
            o1_scr[rows, :] = o
            l1_scr[rows, :] = lse
        return carry

    lax.fori_loop(0, 16 // UNITS_PER_BATCH, body1, 0)

    def window1(cur_ref, prev_ref, blk):
        if blk == 0:
            return cat(prev_ref[...], cur_ref[0:BLOCK])
        return cur_ref[(blk - 1) * BLOCK:(blk + 1) * BLOCK]

    for b0 in range(0, SUPER // BLOCK, UNITS_PER_BATCH):
        blks = list(range(b0, b0 + UNITS_PER_BATCH))
        units = [(q0_ref[blk * BLOCK:(blk + 1) * BLOCK, :], window1(k0c_ref, k0p_ref, blk),
                  window1(v0c_ref, v0p_ref, blk), biases[0][1] if blk == 0 else biases[0][0])
                 for blk in blks]
        for blk, (o0, lse0) in zip(blks, _attn_units(units)):
            rows = slice(blk * BLOCK, (blk + 1) * BLOCK)
            lse1 = l1_scr[rows, :]
            lse2 = l2_scr[rows, :]
            mx = jnp.maximum(jnp.maximum(lse0, lse1), lse2)
            w0 = jnp.exp2(lse0 - mx)
            w1 = jnp.exp2(lse1 - mx)
            w2 = jnp.exp2(lse2 - mx)
            o = (w0 * o0 + w1 * o1_scr[rows, :] + w2 * o2_scr[rows, :]) / (w0 + w1 + w2)
            o_ref[rows, :] = o.astype(BF16)


def _mixer_a(heads, slopes_a, w_ff1, w_ff2, bsz, seq):
    m = bsz * seq
    nsb = seq // SUPER
    n_rows = m // TM_IN
    grid = (bsz, nsb, A_HPG)
    n_steps = bsz * nsb * A_HPG
    assert w_ff1.shape[0] % (16 * n_steps) == 0 and w_ff2.shape[0] % (16 * n_steps) == 0

    def weight_slice(w):
        return pl.BlockSpec((w.shape[0] // n_steps, w.shape[1]), lambda b, s, h: ((b * nsb + s) * A_HPG + h, 0))
    nat = heads.reshape(N_HEADS_OUT, m, HEAD_DIM)
    by4 = heads.reshape(N_HEADS_OUT, n_rows, 4, TM_IN // 4, HEAD_DIM)
    by16 = heads.reshape(N_HEADS_OUT, n_rows, 16, TM_IN // 16, HEAD_DIM)

    def head(role, g, h):
        return (role * N_GROUPS + g) * A_HPG + h

    def nat_cur(role):
        return pl.BlockSpec((None, SUPER, HEAD_DIM), lambda b, s, h: (head(role, 0, h), b * nsb + s, 0))

    def nat_prev(role):
        per = SUPER // BLOCK
        return pl.BlockSpec((None, BLOCK, HEAD_DIM),
                            lambda b, s, h: (head(role, 0, h), jnp.maximum((b * nsb + s) * per - 1, 0), 0))

    def dil_cur(role, g, d):
        return pl.BlockSpec((None, TILES_PER_SUPER, d, TM_IN // d, HEAD_DIM),
                            lambda b, s, h: (head(role, g, h), b * nsb + s, 0, 0, 0))

    def prev4(role):
        return pl.BlockSpec((None, None, 4, BLOCK, HEAD_DIM),
                            lambda b, s, h: (head(role, 1, h),
                                             jnp.maximum((b * nsb + s) * TILES_PER_SUPER - 1, 0), 0, 1, 0))

    def prev16(role):
        return pl.BlockSpec((None, TILES_PER_SUPER, 16, TM_IN // 16, HEAD_DIM),
                            lambda b, s, h: (head(role, 2, h), jnp.maximum(b * nsb + s - 1, 0), 0, 0, 0))

    in_specs = [pl.BlockSpec(memory_space=pltpu.SMEM),
                nat_cur(0), nat_cur(1), nat_prev(1), nat_cur(2), nat_prev(2),
                dil_cur(0, 1, 4), dil_cur(1, 1, 4), prev4(1), dil_cur(2, 1, 4), prev4(2),
                dil_cur(0, 2, 16), dil_cur(1, 2, 16), prev16(1), dil_cur(2, 2, 16), prev16(2),
                weight_slice(w_ff1), weight_slice(w_ff2)]
    return pl.pallas_call(
        _mixer_a_kernel,
        grid=grid,
        in_specs=in_specs,
        out_specs=[pl.BlockSpec((None, SUPER, HEAD_DIM), lambda b, s, h: (h, b * nsb + s, 0)),
                   weight_slice(w_ff1), weight_slice(w_ff2)],
        out_shape=[jax.ShapeDtypeStruct((A_HPG, m, HEAD_DIM), BF16),
                   jax.ShapeDtypeStruct(w_ff1.shape, BF16), jax.ShapeDtypeStruct(w_ff2.shape, BF16)],
        scratch_shapes=[pltpu.VMEM((SUPER, HEAD_DIM), F32) for _ in range(4)],
        compiler_params=pltpu.CompilerParams(
            dimension_semantics=("arbitrary", "arbitrary", "arbitrary"), vmem_limit_bytes=VMEM_LIMIT),
        name="mixer_a",
    )(slopes_a, nat, nat, nat, nat, nat, by4, by4, by4, by4, by4, by16, by16, by16, by16, by16, w_ff1, w_ff2)


def _mixer_b_kernel(slopes_ref, sinks_ref, q_ref, kc_ref, kp_ref, vc_ref, vp_ref, o_ref, *, tiles_per_seq):
    first = (pl.program_id(0) % tiles_per_seq) == 0
    kvh = pl.program_id(1)
    tq = kc_ref.shape[0]

    biases, biases_head = [], []
    for hh in range(B_GROUP):
        bias, ki = _band_bias(slopes_ref[kvh * B_GROUP + hh], B_MAX_BACK)
        biases.append(bias)
        biases_head.append(jnp.where(first, jnp.where(ki < BLOCK, NEG_INF, bias), bias))

    for blk in range(tq // BLOCK):
        if blk == 0:
            kw = jnp.concatenate([kp_ref[...], kc_ref[0:BLOCK]], axis=0)
            vw = jnp.concatenate([vp_ref[...], vc_ref[0:BLOCK]], axis=0)
        else:
            kw = kc_ref[(blk - 1) * BLOCK:(blk + 1) * BLOCK]
            vw = vc_ref[(blk - 1) * BLOCK:(blk + 1) * BLOCK]
        q = q_ref[:, blk * BLOCK:(blk + 1) * BLOCK, :].reshape(B_GROUP * BLOCK, HEAD_DIM)
        s_all = _dot_nt(q, kw)
        ps, sink_terms = [], []
        for hh in range(B_GROUP):
            s = s_all[hh * BLOCK:(hh + 1) * BLOCK] + (biases_head[hh] if blk == 0 else biases[hh])
            sink = sinks_ref[kvh * B_GROUP + hh]
            mrow = jnp.maximum(jnp.max(s, axis=-1, keepdims=True), sink)
            ps.append(jnp.exp2(s - mrow).astype(BF16))
            sink_terms.append(jnp.exp2(sink - mrow))
        o_all = _dot(jnp.concatenate(ps, axis=0), _with_ones(vw))
        for hh in range(B_GROUP):
            rows = slice(hh * BLOCK, (hh + 1) * BLOCK)
            o = o_all[rows, :HEAD_DIM] / (o_all[rows, HEAD_DIM:] + sink_terms[hh])
            o_ref[hh, blk * BLOCK:(blk + 1) * BLOCK, :] = o.astype(BF16)


def _mixer_b(heads, slopes_b, sinks, bsz, seq):
    m = bsz * seq
    tq = TQ_B
    per = tq // BLOCK
    grid = (m // tq, B_KV_HEADS)
    nat = heads.reshape(N_HEADS_OUT, m, HEAD_DIM)

    def cur(head0):
        return pl.BlockSpec((None, tq, HEAD_DIM), lambda i, c: (head0 + c, i, 0))

    def prev(head0):
        return pl.BlockSpec((None, BLOCK, HEAD_DIM), lambda i, c: (head0 + c, jnp.maximum(i * per - 1, 0), 0))

    smem = pl.BlockSpec(memory_space=pltpu.SMEM)
    return pl.pallas_call(
        functools.partial(_mixer_b_kernel, tiles_per_seq=seq // tq),
        grid=grid,
        in_specs=[smem, smem,
                  pl.BlockSpec((B_GROUP, tq, HEAD_DIM), lambda i, c: (QB_HEAD0 // B_GROUP + c, i, 0)),
                  cur(KB_HEAD0), prev(KB_HEAD0), cur(VB_HEAD0), prev(VB_HEAD0)],
        out_specs=pl.BlockSpec((B_GROUP, tq, HEAD_DIM), lambda i, c: (c, i, 0)),
        out_shape=jax.ShapeDtypeStruct((B_Q_HEADS, m, HEAD_DIM), BF16),
        compiler_params=pltpu.CompilerParams(
            dimension_semantics=("arbitrary", "arbitrary"), vmem_limit_bytes=VMEM_LIMIT),
        name="mixer_b",
    )(slopes_b, sinks, nat, nat, nat, nat, nat)


def _merge_kernel(oa_ref, ob_ref, gate_ref, wa_ref, wb_ref, out_ref):
    oa = jnp.concatenate([oa_ref[hh] for hh in range(A_HPG)], axis=-1)
    ob = jnp.concatenate([ob_ref[hh] for hh in range(B_Q_HEADS)], axis=-1)
    per = D_MODEL // GATE_W
    ga = jnp.concatenate([gate_ref[GATE_UNIT0 + c] for c in range(per)], axis=-1).astype(F32)
    gb = jnp.concatenate([gate_ref[GATE_UNIT0 + per + c] for c in range(per)], axis=-1).astype(F32)
    merged = ga * _dot(oa, wa_ref[...]) + gb * _dot(ob, wb_ref[...])
    out_ref[...] = merged.astype(BF16)


def _merge(oa, ob, gates, wa, wb):
    m = gates.shape[1]
    tm = TM_MERGE
    resident = dict(pipeline_mode=pl.Buffered(1))
    return pl.pallas_call(
        _merge_kernel,
        grid=(m // tm,),
        in_specs=[pl.BlockSpec((A_HPG, tm, HEAD_DIM), lambda i: (0, i, 0)),
                  pl.BlockSpec((B_Q_HEADS, tm, HEAD_DIM), lambda i: (0, i, 0)),
                  pl.BlockSpec((GATE_UNITS, tm, GATE_W), lambda i: (0, i, 0)),
                  pl.BlockSpec(wa.shape, lambda i: (0, 0), **resident),
                  pl.BlockSpec(wb.shape, lambda i: (0, 0), **resident)],
        out_specs=pl.BlockSpec((tm, D_MODEL), lambda i: (i, 0)),
        out_shape=jax.ShapeDtypeStruct((m, D_MODEL), BF16),
        compiler_params=pltpu.CompilerParams(
            dimension_semantics=("arbitrary",), vmem_limit_bytes=VMEM_LIMIT),
        name="gated_merge",
    )(oa, ob, gates, wa, wb)


def _ffn_kernel(x_ref, mg_ref, wo_ref, g2_ref, w1_ref, w2_ref, out_ref, h2_scr):
    @pl.when(pl.program_id(1) == 0)
    def _():
        x1 = x_ref[...] + _dot(mg_ref[...], wo_ref[...])
        out_ref[...] = x1
        h2_scr[...] = _rms_scale(x1, g2_ref[...]).astype(BF16)

    z = _dot(h2_scr[...], w1_ref[...])
    act = jnp.square(jnp.maximum(z, 0.0)).astype(BF16)
    out_ref[...] += _dot(act, w2_ref[...])


def _ffn(x2, merged, wo, g2, w1, w2):
    m = x2.shape[0]
    tm, tf = TM_FFN, TF_FFN
    return pl.pallas_call(
        _ffn_kernel,
        grid=(m // tm, D_FF // tf),
        in_specs=[pl.BlockSpec((tm, D_MODEL), lambda i, f: (i, 0)),
                  pl.BlockSpec((tm, D_MODEL), lambda i, f: (i, 0)),
                  pl.BlockSpec(wo.shape, lambda i, f: (0, 0), pipeline_mode=pl.Buffered(1)),
                  pl.BlockSpec((1, D_MODEL), lambda i, f: (0, 0)),
                  pl.BlockSpec((D_MODEL, tf), lambda i, f: (0, f)),
                  pl.BlockSpec((tf, D_MODEL), lambda i, f: (f, 0))],
        out_specs=pl.BlockSpec((tm, D_MODEL), lambda i, f: (i, 0)),
        out_shape=jax.ShapeDtypeStruct((m, D_MODEL), F32),
        scratch_shapes=[pltpu.VMEM((tm, D_MODEL), BF16)],
        compiler_params=pltpu.CompilerParams(
            dimension_semantics=("arbitrary", "arbitrary"), vmem_limit_bytes=VMEM_LIMIT),
        name="out_proj_ffn",
    )(x2, merged, wo, g2, w1, w2)


def kernel(x, norm1_g, w_in, q_norm_a, k_norm_a, q_norm_b, k_norm_b, sinks_b, w_branch_a, w_branch_b,
           w_out, norm2_g, w_ff1, w_ff2):
    bsz, seq, d_model = x.shape
    depth = w_in.shape[0]
    assert d_model == D_MODEL and w_in.shape[2] == IN_COLS
    assert seq % SUPER == 0 and seq % TQ_B == 0 and SUPER % TM_IN == 0
    m = bsz * seq
    slopes = _alibi_slopes() * np.float32(LOG2E)
    slopes_b = jnp.asarray(slopes[:B_Q_HEADS])
    slopes_a = jnp.asarray(slopes[B_Q_HEADS:])
    scale = HEAD_DIM ** -0.5 * LOG2E

    x2 = x.reshape(m, D_MODEL)
    for l in range(depth):
        qk_gains = jnp.stack([q_norm_a[l] * scale, k_norm_a[l], q_norm_b[l] * scale, k_norm_b[l]])
        w_tail = w_in[l][:, TAIL_TILE * TN_IN:].astype(BF16)
        heads, gates = _in_projection(x2, norm1_g[l].reshape(1, D_MODEL), w_in[l].astype(BF16), w_tail, qk_gains)
        oa, w1_bf16, w2_bf16 = _mixer_a(heads, slopes_a, w_ff1[l], w_ff2[l], bsz, seq)
        ob = _mixer_b(heads, slopes_b, sinks_b[l] * LOG2E, bsz, seq)
        merged = _merge(oa, ob, gates, w_branch_a[l].astype(BF16), w_branch_b[l].astype(BF16))
        x2 = _ffn(x2, merged, w_out[l].astype(BF16), norm2_g[l].reshape(1, D_MODEL),
                  w1_bf16, w2_bf16)
    return x2.reshape(bsz, seq, D_MODEL)
```

```python
import functools

import numpy as np
import jax
import jax.numpy as jnp
from jax import lax
from jax.experimental import pallas as pl
from jax.experimental.pallas import tpu as pltpu

D_MODEL = 2048
HEAD_DIM = 128
BLOCK = 128
EPS = 1e-6
DILATIONS = (1, 4, 16)
A_MAX_BACK = 128
N_GROUPS = 3
A_HPG = 8
A_WIDTH = N_GROUPS * A_HPG * HEAD_DIM
B_Q_HEADS = 16
B_KV_HEADS = 2
B_GROUP = B_Q_HEADS // B_KV_HEADS
B_MAX_BACK = 127
D_FF = 4 * D_MODEL
N_ATTN_HEADS = B_Q_HEADS + N_GROUPS * A_HPG
IN_COLS = 3 * A_WIDTH + B_Q_HEADS * HEAD_DIM + 2 * B_KV_HEADS * HEAD_DIM + 2 * D_MODEL

SUPER = BLOCK * DILATIONS[-1]
NEG_INF = float("-inf")
LOG2E = 1.4426950408889634
BF16 = jnp.bfloat16
F32 = jnp.float32

VMEM_LIMIT = 56 * 1024 * 1024

TM_IN = 1024
TN_IN = 1024
HEADS_PER_TILE = TN_IN // HEAD_DIM
A_TILES = 3 * N_GROUPS
QB_TILE0 = A_TILES
QB_TILES = B_Q_HEADS // HEADS_PER_TILE
KVB_TILE = QB_TILE0 + QB_TILES
N_TILES_IN = -(-IN_COLS // TN_IN)
TAIL_TILE = N_TILES_IN - 1
TAIL_COLS = IN_COLS - TAIL_TILE * TN_IN
N_HEAD_TILES = KVB_TILE + 1
N_HEADS_OUT = N_HEAD_TILES * HEADS_PER_TILE
QB_HEAD0 = QB_TILE0 * HEADS_PER_TILE
KB_HEAD0 = KVB_TILE * HEADS_PER_TILE
VB_HEAD0 = KB_HEAD0 + B_KV_HEADS
GATE_W = TN_IN // 2
GATE_UNITS = 2 * (N_TILES_IN - KVB_TILE)
GATE_UNIT0 = 1

TQ_B = 1024
TM_MERGE = 512
TM_FFN = 512
TF_FFN = 1024


def _alibi_slopes():
    i = np.arange(1, N_ATTN_HEADS + 1, dtype=np.float32)
    return (2.0 ** (-8.0 * i / N_ATTN_HEADS)).astype(np.float32)


def _rms_scale(a, gain):
    ms = jnp.mean(a * a, axis=-1, keepdims=True)
    return a * lax.rsqrt(ms + EPS) * gain


def _dot(a, b):
    return jnp.dot(a, b, preferred_element_type=F32)


def _dot_nt(a, b):
    return lax.dot_general(a, b, (((1,), (1,)), ((), ())), preferred_element_type=F32)


def _inproj_kernel(x_ref, g1_ref, w_ref, wtail_ref, qkg_ref, heads_ref, gate_ref, h_scr, acc_scr, tmp_scr, *, n_steps):
    s = pl.program_id(0)
    p = s % N_TILES_IN
    t = jnp.maximum(s - 1, 0) % N_TILES_IN - 1
    has_matmul = s < n_steps - 1
    tm = x_ref.shape[0]

    def matmul_tile():
        acc = _dot(h_scr[...], w_ref[...])
        for c in range(HEADS_PER_TILE):
            acc_scr[c] = acc[:, c * HEAD_DIM:(c + 1) * HEAD_DIM]

    def matmul_tail():
        acc = _dot(h_scr[...], wtail_ref[...])
        for c in range(TAIL_COLS // HEAD_DIM):
            acc_scr[c] = acc[:, c * HEAD_DIM:(c + 1) * HEAD_DIM]

    def write_head(c, d, gain):
        def emit(a, res, rows):
            if gain is not None:
                a = _rms_scale(a, gain)
            heads_ref[c, res * rows:(res + 1) * rows, :] = a.astype(BF16)

        if d == 1:
            emit(acc_scr[c], 0, tm)
        elif d == 4:
            for res in range(4):
                emit(acc_scr[c, pl.ds(res, tm // 4, stride=4), :], res, tm // 4)
        else:
            tmp = tmp_scr.at[c % 2]
            for r4 in range(4):
                tmp[r4 * (tm // 4):(r4 + 1) * (tm // 4), :] = acc_scr[c, pl.ds(r4, tm // 4, stride=4), :]
            for r4 in range(4):
                for q in range(4):
                    emit(tmp[pl.ds(r4 * (tm // 4) + q, tm // 16, stride=4), :], r4 + 4 * q, tm // 16)

    def write_gates(c0, unit):
        for c in range(GATE_W // HEAD_DIM):
            z = acc_scr[c0 + c]
            gate_ref[unit, :, c * HEAD_DIM:(c + 1) * HEAD_DIM] = (0.5 + 0.5 * jnp.tanh(0.5 * z)).astype(BF16)

    def write_gate_tile():
        write_gates(0, 0)
        write_gates(GATE_W // HEAD_DIM, 1)

    def write_gate_tail():
        write_gates(0, 0)
        gate_ref[1] = jnp.zeros(gate_ref.shape[1:], BF16)

    @pl.when(has_matmul & (p == 0))
    def _():
        @pl.when(s > 0)
        def _():
            write_gate_tile()
        rows = 256
        for c in range(tm // rows):
            x = x_ref[c * rows:(c + 1) * rows, :]
            h_scr[c * rows:(c + 1) * rows, :] = _rms_scale(x, g1_ref[...]).astype(BF16)
        matmul_tail()

    @pl.when(s == n_steps - 1)
    def _():
        write_gate_tile()

    @pl.when(has_matmul & (p == 1))
    def _():
        write_gate_tail()
        matmul_tile()

    mid_row = has_matmul & (p > 1)
    role = t // N_GROUPS
    group = t % N_GROUPS
    is_a = mid_row & (t < A_TILES)

    for g, d in enumerate(DILATIONS):
        @pl.when(is_a & (group == g) & (role < 2))
        def _(d=d):
            gain = jnp.where(role == 0, qkg_ref[0:1, :], qkg_ref[1:2, :])
            for c in range(HEADS_PER_TILE):
                write_head(c, d, gain)
            matmul_tile()

        @pl.when(is_a & (group == g) & (role == 2))
        def _(d=d):
            for c in range(HEADS_PER_TILE):
                write_head(c, d, None)
            matmul_tile()

    @pl.when(mid_row & (t >= QB_TILE0) & (t < KVB_TILE))
    def _():
        for c in range(HEADS_PER_TILE):
            write_head(c, 1, qkg_ref[2:3, :])
        matmul_tile()

    @pl.when(mid_row & (t == KVB_TILE))
    def _():
        for c in range(B_KV_HEADS):
            write_head(c, 1, qkg_ref[3:4, :])
        for c in range(B_KV_HEADS, HEADS_PER_TILE):
            write_head(c, 1, None)
        write_gate_tile()
        matmul_tile()

    @pl.when(mid_row & (t > KVB_TILE))
    def _():
        write_gate_tile()
        matmul_tile()


def _in_projection(x2, g1, w, w_tail, qk_gains):
    m = x2.shape[0]
    tm = TM_IN
    n_rows = m // tm
    n_steps = n_rows * N_TILES_IN + 1

    def row_of(s):
        return jnp.maximum(s - 1, 0) // N_TILES_IN

    def tile_of(s):
        return jnp.maximum(s - 1, 0) % N_TILES_IN - 1

    out_shapes = [
        jax.ShapeDtypeStruct((N_HEADS_OUT, n_rows, tm, HEAD_DIM), BF16),
        jax.ShapeDtypeStruct((GATE_UNITS, m, GATE_W), BF16),
    ]
    out_specs = [
        pl.BlockSpec((HEADS_PER_TILE, None, tm, HEAD_DIM),
                     lambda s: (jnp.clip(tile_of(s), 0, KVB_TILE), row_of(s), 0, 0)),
        pl.BlockSpec((2, tm, GATE_W),
                     lambda s: (jnp.where(tile_of(s) >= KVB_TILE, tile_of(s) - KVB_TILE, TAIL_TILE - KVB_TILE),
                                row_of(s), 0)),
    ]
    in_specs = [
        pl.BlockSpec((tm, D_MODEL), lambda s: (jnp.minimum(s // N_TILES_IN, n_rows - 1), 0)),
        pl.BlockSpec((1, D_MODEL), lambda s: (0, 0)),
        pl.BlockSpec((D_MODEL, TN_IN),
                     lambda s: (0, jnp.where(s < n_steps - 1, jnp.maximum(s % N_TILES_IN - 1, 0),
                                             TAIL_TILE - 1))),
        pl.BlockSpec((D_MODEL, TAIL_COLS), lambda s: (0, 0), pipeline_mode=pl.Buffered(1)),
        pl.BlockSpec((4, HEAD_DIM), lambda s: (0, 0)),
    ]
    return pl.pallas_call(
        functools.partial(_inproj_kernel, n_steps=n_steps),
        grid=(n_steps,),
        in_specs=in_specs,
        out_specs=out_specs,
        out_shape=out_shapes,
        scratch_shapes=[pltpu.VMEM((tm, D_MODEL), BF16),
                        pltpu.VMEM((HEADS_PER_TILE, tm, HEAD_DIM), F32),
                        pltpu.VMEM((2, tm, HEAD_DIM), F32)],
        compiler_params=pltpu.CompilerParams(
            dimension_semantics=("arbitrary",), vmem_limit_bytes=VMEM_LIMIT),
        name="in_projection",
    )(x2, g1, w, w_tail, qk_gains)


def _band_bias(slope_times_stride, max_back):
    qi = lax.broadcasted_iota(jnp.int32, (BLOCK, 2 * BLOCK), 0) + BLOCK
    ki = lax.broadcasted_iota(jnp.int32, (BLOCK, 2 * BLOCK), 1)
    rel = qi - ki
    valid = (rel >= 0) & (rel <= max_back)
    bias = jnp.where(valid, -slope_times_stride * rel.astype(F32), NEG_INF)
    return bias, ki


def _with_ones(vw):
    return jnp.concatenate([vw, jnp.ones(vw.shape, vw.dtype)], axis=-1)


def _attn_units(units):
    scores = [_dot_nt(q, kw) + bias for q, kw, _, bias in units]
    maxes = [jnp.max(s, axis=-1, keepdims=True) for s in scores]
    probs = [jnp.exp2(s - m).astype(BF16) for s, m in zip(scores, maxes)]
    outs = [_dot(p, _with_ones(u[2])) for p, u in zip(probs, units)]
    results = []
    for o_l, m in zip(outs, maxes):
        o, l = o_l[:, :HEAD_DIM], o_l[:, HEAD_DIM:]
        results.append((o / l, m + jnp.log2(l)))
    return results


UNITS_PER_BATCH = 16
TILES_PER_SUPER = SUPER // TM_IN


def _mixer_a_kernel(slopes_ref, q0_ref, k0c_ref, v0c_ref, q1_ref, k1c_ref, v1c_ref, q2_ref, k2c_ref, v2c_ref,
                    w1_ref, w2_ref, o_ref, w1_out_ref, w2_out_ref,
                    o1_scr, l1_scr, o2_scr, l2_scr, k0p_ref, v0p_ref, k1p_ref, v1p_ref, k2p_ref, v2p_ref):
    w1_out_ref[...] = w1_ref[...].astype(BF16)
    w2_out_ref[...] = w2_ref[...].astype(BF16)

    h = pl.program_id(1)
    first = pl.program_id(2) == 0
    prev_refs = (k0p_ref, v0p_ref, k1p_ref, v1p_ref, k2p_ref, v2p_ref)

    @pl.when(first)
    def _():
        for ref in prev_refs:
            ref[...] = jnp.zeros(ref.shape, ref.dtype)

    biases = []
    for g, d in enumerate(DILATIONS):
        bias, ki = _band_bias(slopes_ref[g * A_HPG + h] * float(d), A_MAX_BACK)
        bias_head = jnp.where(first, jnp.where(ki < BLOCK, NEG_INF, bias), bias)
        biases.append((bias, bias_head))

    def cat(*parts):
        return jnp.concatenate(parts, axis=0)

    def block16(ref, r):
        return cat(ref[0, r], ref[1, r])

    def body2(it, carry):
        rs = [it * UNITS_PER_BATCH + u for u in range(UNITS_PER_BATCH)]
        units = [(block16(q2_ref, r), cat(block16(k2p_ref, r), block16(k2c_ref, r)),
                  cat(block16(v2p_ref, r), block16(v2c_ref, r)), biases[2][1]) for r in rs]
        for r, (o, lse) in zip(rs, _attn_units(units)):
            o2_scr[pl.ds(r, BLOCK, stride=16), :] = o
            l2_scr[pl.ds(r, BLOCK, stride=16), :] = lse
        return carry

    lax.fori_loop(0, 16 // UNITS_PER_BATCH, body2, 0)

    def block4(ref, r, blk):
        half = (blk % 2) * BLOCK
        return ref[blk // 2, r, half:half + BLOCK]

    def window4(cur_ref, prev_ref, r, blk):
        prev = prev_ref[r] if blk == 0 else block4(cur_ref, r, blk - 1)
        return cat(prev, block4(cur_ref, r, blk))

    def body1(it, carry):
        per = UNITS_PER_BATCH // 4
        rb = [(it * per + u // 4, u % 4) for u in range(UNITS_PER_BATCH)]
        units = [(block4(q1_ref, r, blk), window4(k1c_ref, k1p_ref, r, blk), window4(v1c_ref, v1p_ref, r, blk),
                  biases[1][1] if blk == 0 else biases[1][0]) for r, blk in rb]
        for (r, blk), (o, lse) in zip(rb, _attn_units(units)):
            rows = pl.ds(blk * BLOCK * 4 + r, BLOCK, stride=4)
            o1_scr[rows, :] = o
            l1_scr[rows, :] = lse
        return carry

    lax.fori_loop(0, 16 // UNITS_PER_BATCH, body1, 0)

    def window1(cur_ref, prev_ref, blk):
        if blk == 0:
            return cat(prev_ref[...], cur_ref[0:BLOCK])
        return cur_ref[(blk - 1) * BLOCK:(blk + 1) * BLOCK]

    for b0 in range(0, SUPER // BLOCK, UNITS_PER_BATCH):
        blks = list(range(b0, b0 + UNITS_PER_BATCH))
        units = [(q0_ref[blk * BLOCK:(blk + 1) * BLOCK, :], window1(k0c_ref, k0p_ref, blk),
                  window1(v0c_ref, v0p_ref, blk), biases[0][1] if blk == 0 else biases[0][0])
                 for blk in blks]
        for blk, (o0, lse0) in zip(blks, _attn_units(units)):
            rows = slice(blk * BLOCK, (blk + 1) * BLOCK)
            lse1 = l1_scr[rows, :]
            lse2 = l2_scr[rows, :]
            mx = jnp.maximum(jnp.maximum(lse0, lse1), lse2)
            w0 = jnp.exp2(lse0 - mx)
            w1 = jnp.exp2(lse1 - mx)
            w2 = jnp.exp2(lse2 - mx)
            o = (w0 * o0 + w1 * o1_scr[rows, :] + w2 * o2_scr[rows, :]) / (w0 + w1 + w2)
            o_ref[rows, :] = o.astype(BF16)

    k0p_ref[...] = k0c_ref[SUPER - BLOCK:SUPER]
    v0p_ref[...] = v0c_ref[SUPER - BLOCK:SUPER]
    k1p_ref[...] = k1c_ref[TILES_PER_SUPER - 1, :, TM_IN // 4 - BLOCK:TM_IN // 4]
    v1p_ref[...] = v1c_ref[TILES_PER_SUPER - 1, :, TM_IN // 4 - BLOCK:TM_IN // 4]
    k2p_ref[...] = k2c_ref[...]
    v2p_ref[...] = v2c_ref[...]


def _mixer_a(heads, slopes_a, w_ff1, w_ff2, bsz, seq):
    m = bsz * seq
    nsb = seq // SUPER
    n_rows = m // TM_IN
    grid = (bsz, A_HPG, nsb)
    n_steps = bsz * nsb * A_HPG
    assert w_ff1.shape[0] % (16 * n_steps) == 0 and w_ff2.shape[0] % (16 * n_steps) == 0

    def weight_slice(w):
        return pl.BlockSpec((w.shape[0] // n_steps, w.shape[1]), lambda b, h, s: ((b * A_HPG + h) * nsb + s, 0))
    nat = heads.reshape(N_HEADS_OUT, m, HEAD_DIM)
    by4 = heads.reshape(N_HEADS_OUT, n_rows, 4, TM_IN // 4, HEAD_DIM)
    by16 = heads.reshape(N_HEADS_OUT, n_rows, 16, TM_IN // 16, HEAD_DIM)

    def head(role, g, h):
        return (role * N_GROUPS + g) * A_HPG + h

    def nat_cur(role):
        return pl.BlockSpec((None, SUPER, HEAD_DIM), lambda b, h, s: (head(role, 0, h), b * nsb + s, 0))

    def dil_cur(role, g, d):
        return pl.BlockSpec((None, TILES_PER_SUPER, d, TM_IN // d, HEAD_DIM),
                            lambda b, h, s: (head(role, g, h), b * nsb + s, 0, 0, 0))

    in_specs = [pl.BlockSpec(memory_space=pltpu.SMEM),
                nat_cur(0), nat_cur(1), nat_cur(2),
                dil_cur(0, 1, 4), dil_cur(1, 1, 4), dil_cur(2, 1, 4),
                dil_cur(0, 2, 16), dil_cur(1, 2, 16), dil_cur(2, 2, 16),
                weight_slice(w_ff1), weight_slice(w_ff2)]
    prev_shapes = [(BLOCK, HEAD_DIM), (4, BLOCK, HEAD_DIM), (TILES_PER_SUPER, 16, TM_IN // 16, HEAD_DIM)]
    return pl.pallas_call(
        _mixer_a_kernel,
        grid=grid,
        in_specs=in_specs,
        out_specs=[pl.BlockSpec((None, SUPER, HEAD_DIM), lambda b, h, s: (h, b * nsb + s, 0)),
                   weight_slice(w_ff1), weight_slice(w_ff2)],
        out_shape=[jax.ShapeDtypeStruct((A_HPG, m, HEAD_DIM), BF16),
                   jax.ShapeDtypeStruct(w_ff1.shape, BF16), jax.ShapeDtypeStruct(w_ff2.shape, BF16)],
        scratch_shapes=([pltpu.VMEM((SUPER, HEAD_DIM), F32) for _ in range(4)]
                        + [pltpu.VMEM(shape, BF16) for shape in prev_shapes for _ in range(2)]),
        compiler_params=pltpu.CompilerParams(
            dimension_semantics=("arbitrary", "arbitrary", "arbitrary"), vmem_limit_bytes=VMEM_LIMIT),
        name="mixer_a",
    )(slopes_a, nat, nat, nat, by4, by4, by4, by16, by16, by16, w_ff1, w_ff2)


def _mixer_b_kernel(slopes_ref, sinks_ref, q_ref, kc_ref, kp_ref, vc_ref, vp_ref, *rest, tiles_per_seq):
    n_w = (len(rest) - 1) // 2
    w_refs, o_ref, w_out_refs = rest[:n_w], rest[n_w], rest[n_w + 1:]
    for w_ref, w_out_ref in zip(w_refs, w_out_refs):
        w_out_ref[...] = w_ref[...].astype(BF16)

    first = (pl.program_id(0) % tiles_per_seq) == 0
    kvh = pl.program_id(1)
    tq = kc_ref.shape[0]

    biases, biases_head = [], []
    for hh in range(B_GROUP):
        bias, ki = _band_bias(slopes_ref[kvh * B_GROUP + hh], B_MAX_BACK)
        biases.append(bias)
        biases_head.append(jnp.where(first, jnp.where(ki < BLOCK, NEG_INF, bias), bias))

    for blk in range(tq // BLOCK):
        if blk == 0:
            kw = jnp.concatenate([kp_ref[...], kc_ref[0:BLOCK]], axis=0)
            vw = jnp.concatenate([vp_ref[...], vc_ref[0:BLOCK]], axis=0)
        else:
            kw = kc_ref[(blk - 1) * BLOCK:(blk + 1) * BLOCK]
            vw = vc_ref[(blk - 1) * BLOCK:(blk + 1) * BLOCK]
        q = q_ref[:, blk * BLOCK:(blk + 1) * BLOCK, :].reshape(B_GROUP * BLOCK, HEAD_DIM)
        s_all = _dot_nt(q, kw)
        ps, sink_terms = [], []
        for hh in range(B_GROUP):
            s = s_all[hh * BLOCK:(hh + 1) * BLOCK] + (biases_head[hh] if blk == 0 else biases[hh])
            sink = sinks_ref[kvh * B_GROUP + hh]
            mrow = jnp.maximum(jnp.max(s, axis=-1, keepdims=True), sink)
            ps.append(jnp.exp2(s - mrow).astype(BF16))
            sink_terms.append(jnp.exp2(sink - mrow))
        o_all = _dot(jnp.concatenate(ps, axis=0), _with_ones(vw))
        for hh in range(B_GROUP):
            rows = slice(hh * BLOCK, (hh + 1) * BLOCK)
            o = o_all[rows, :HEAD_DIM] / (o_all[rows, HEAD_DIM:] + sink_terms[hh])
            o_ref[hh, blk * BLOCK:(blk + 1) * BLOCK, :] = o.astype(BF16)


def _mixer_b(heads, slopes_b, sinks, weights, bsz, seq):
    m = bsz * seq
    tq = TQ_B
    per = tq // BLOCK
    grid = (m // tq, B_KV_HEADS)
    n_steps = grid[0] * grid[1]
    assert all(w.shape[0] % (16 * n_steps) == 0 for w in weights)
    nat = heads.reshape(N_HEADS_OUT, m, HEAD_DIM)

    def weight_slice(w):
        return pl.BlockSpec((w.shape[0] // n_steps, w.shape[1]), lambda i, c: (i * B_KV_HEADS + c, 0))

    def cur(head0):
        return pl.BlockSpec((None, tq, HEAD_DIM), lambda i, c: (head0 + c, i, 0))

    def prev(head0):
        return pl.BlockSpec((None, BLOCK, HEAD_DIM), lambda i, c: (head0 + c, jnp.maximum(i * per - 1, 0), 0))

    smem = pl.BlockSpec(memory_space=pltpu.SMEM)
    return pl.pallas_call(
        functools.partial(_mixer_b_kernel, tiles_per_seq=seq // tq),
        grid=grid,
        in_specs=[smem, smem,
                  pl.BlockSpec((B_GROUP, tq, HEAD_DIM), lambda i, c: (QB_HEAD0 // B_GROUP + c, i, 0)),
                  cur(KB_HEAD0), prev(KB_HEAD0), cur(VB_HEAD0), prev(VB_HEAD0)]
                 + [weight_slice(w) for w in weights],
        out_specs=[pl.BlockSpec((B_GROUP, tq, HEAD_DIM), lambda i, c: (c, i, 0))]
                  + [weight_slice(w) for w in weights],
        out_shape=[jax.ShapeDtypeStruct((B_Q_HEADS, m, HEAD_DIM), BF16)]
                  + [jax.ShapeDtypeStruct(w.shape, BF16) for w in weights],
        compiler_params=pltpu.CompilerParams(
            dimension_semantics=("arbitrary", "arbitrary"), vmem_limit_bytes=VMEM_LIMIT),
        name="mixer_b",
    )(slopes_b, sinks, nat, nat, nat, nat, nat, *weights)


def _merge_kernel(oa_ref, ob_ref, gate_ref, wa_ref, wb_ref, out_ref):
    oa = jnp.concatenate([oa_ref[hh] for hh in range(A_HPG)], axis=-1)
    ob = jnp.concatenate([ob_ref[hh] for hh in range(B_Q_HEADS)], axis=-1)
    per = D_MODEL // GATE_W
    ga = jnp.concatenate([gate_ref[GATE_UNIT0 + c] for c in range(per)], axis=-1).astype(F32)
    gb = jnp.concatenate([gate_ref[GATE_UNIT0 + per + c] for c in range(per)], axis=-1).astype(F32)
    merged = ga * _dot(oa, wa_ref[...]) + gb * _dot(ob, wb_ref[...])
    out_ref[...] = merged.astype(BF16)


def _merge(oa, ob, gates, wa, wb):
    m = gates.shape[1]
    tm = TM_MERGE
    resident = dict(pipeline_mode=pl.Buffered(1))
    return pl.pallas_call(
        _merge_kernel,
        grid=(m // tm,),
        in_specs=[pl.BlockSpec((A_HPG, tm, HEAD_DIM), lambda i: (0, i, 0)),
                  pl.BlockSpec((B_Q_HEADS, tm, HEAD_DIM), lambda i: (0, i, 0)),
                  pl.BlockSpec((GATE_UNITS, tm, GATE_W), lambda i: (0, i, 0)),
                  pl.BlockSpec(wa.shape, lambda i: (0, 0), **resident),
                  pl.BlockSpec(wb.shape, lambda i: (0, 0), **resident)],
        out_specs=pl.BlockSpec((tm, D_MODEL), lambda i: (i, 0)),
        out_shape=jax.ShapeDtypeStruct((m, D_MODEL), BF16),
        compiler_params=pltpu.CompilerParams(
            dimension_semantics=("arbitrary",), vmem_limit_bytes=VMEM_LIMIT),
        name="gated_merge",
    )(oa, ob, gates, wa, wb)


def _ffn_kernel(x_ref, mg_ref, wo_ref, g2_ref, w1_ref, w2_ref, out_ref, h2_scr):
    @pl.when(pl.program_id(1) == 0)
    def _():
        x1 = x_ref[...] + _dot(mg_ref[...], wo_ref[...])
        out_ref[...] = x1
        h2_scr[...] = _rms_scale(x1, g2_ref[...]).astype(BF16)

    z = _dot(h2_scr[...], w1_ref[...])
    act = jnp.square(jnp.maximum(z, 0.0)).astype(BF16)
    out_ref[...] += _dot(act, w2_ref[...])


def _ffn(x2, merged, wo, g2, w1, w2):
    m = x2.shape[0]
    tm, tf = TM_FFN, TF_FFN
    return pl.pallas_call(
        _ffn_kernel,
        grid=(m // tm, D_FF // tf),
        in_specs=[pl.BlockSpec((tm, D_MODEL), lambda i, f: (i, 0)),
                  pl.BlockSpec((tm, D_MODEL), lambda i, f: (i, 0)),
                  pl.BlockSpec(wo.shape, lambda i, f: (0, 0), pipeline_mode=pl.Buffered(1)),
                  pl.BlockSpec((1, D_MODEL), lambda i, f: (0, 0)),
                  pl.BlockSpec((D_MODEL, tf), lambda i, f: (0, f)),
                  pl.BlockSpec((tf, D_MODEL), lambda i, f: (f, 0))],
        out_specs=pl.BlockSpec((tm, D_MODEL), lambda i, f: (i, 0)),
        out_shape=jax.ShapeDtypeStruct((m, D_MODEL), F32),
        scratch_shapes=[pltpu.VMEM((tm, D_MODEL), BF16)],
        compiler_params=pltpu.CompilerParams(
            dimension_semantics=("arbitrary", "arbitrary"), vmem_limit_bytes=VMEM_LIMIT),
        name="out_proj_ffn",
    )(x2, merged, wo, g2, w1, w2)


def kernel(x, norm1_g, w_in, q_norm_a, k_norm_a, q_norm_b, k_norm_b, sinks_b, w_branch_a, w_branch_b,
           w_out, norm2_g, w_ff1, w_ff2):
    bsz, seq, d_model = x.shape
    depth = w_in.shape[0]
    assert d_model == D_MODEL and w_in.shape[2] == IN_COLS
    assert seq % SUPER == 0 and seq % TQ_B == 0 and SUPER % TM_IN == 0
    m = bsz * seq
    slopes = _alibi_slopes() * np.float32(LOG2E)
    slopes_b = jnp.asarray(slopes[:B_Q_HEADS])
    slopes_a = jnp.asarray(slopes[B_Q_HEADS:])
    scale = HEAD_DIM ** -0.5 * LOG2E

    x2 = x.reshape(m, D_MODEL)
    for l in range(depth):
        qk_gains = jnp.stack([q_norm_a[l] * scale, k_norm_a[l], q_norm_b[l] * scale, k_norm_b[l]])
        w_tail = w_in[l][:, TAIL_TILE * TN_IN:].astype(BF16)
        heads, gates = _in_projection(x2, norm1_g[l].reshape(1, D_MODEL), w_in[l].astype(BF16), w_tail, qk_gains)
        oa, w1_bf16, w2_bf16 = _mixer_a(heads, slopes_a, w_ff1[l], w_ff2[l], bsz, seq)
        ob, wa_bf16, wb_bf16, wo_bf16 = _mixer_b(heads, slopes_b, sinks_b[l] * LOG2E,
                                                 (w_branch_a[l], w_branch_b[l], w_out[l]), bsz, seq)
        merged = _merge(oa, ob, gates, wa_bf16, wb_bf16)
        x2 = _ffn(x2, merged, wo_bf16, norm2_g[l].reshape(1, D_MODEL),
                  w1_bf16, w2_bf16)
    return x2.reshape(bsz, seq, D_MODEL)
```

```python
import functools

import numpy as np
import jax
import jax.numpy as jnp
from jax import lax
from jax.experimental import pallas as pl
from jax.experimental.pallas import tpu as pltpu

D_MODEL = 2048
HEAD_DIM = 128
BLOCK = 128
EPS = 1e-6
DILATIONS = (1, 4, 16)
A_MAX_BACK = 128
N_GROUPS = 3
A_HPG = 8
A_WIDTH = N_GROUPS * A_HPG * HEAD_DIM
B_Q_HEADS = 16
B_KV_HEADS = 2
B_GROUP = B_Q_HEADS // B_KV_HEADS
B_MAX_BACK = 127
D_FF = 4 * D_MODEL
N_ATTN_HEADS = B_Q_HEADS + N_GROUPS * A_HPG
IN_COLS = 3 * A_WIDTH + B_Q_HEADS * HEAD_DIM + 2 * B_KV_HEADS * HEAD_DIM + 2 * D_MODEL

SUPER = BLOCK * DILATIONS[-1]
NEG_INF = float("-inf")
LOG2E = 1.4426950408889634
BF16 = jnp.bfloat16
F32 = jnp.float32

VMEM_LIMIT = 56 * 1024 * 1024

TM_IN = 1024
TN_IN = 1024
HEADS_PER_TILE = TN_IN // HEAD_DIM
A_TILES = 3 * N_GROUPS
QB_TILE0 = A_TILES
QB_TILES = B_Q_HEADS // HEADS_PER_TILE
KVB_TILE = QB_TILE0 + QB_TILES
N_TILES_IN = -(-IN_COLS // TN_IN)
TAIL_TILE = N_TILES_IN - 1
TAIL_COLS = IN_COLS - TAIL_TILE * TN_IN
N_HEAD_TILES = KVB_TILE + 1
N_HEADS_OUT = N_HEAD_TILES * HEADS_PER_TILE
QB_HEAD0 = QB_TILE0 * HEADS_PER_TILE
KB_HEAD0 = KVB_TILE * HEADS_PER_TILE
VB_HEAD0 = KB_HEAD0 + B_KV_HEADS
GATE_W = TN_IN // 2
GATE_UNITS = 2 * (N_TILES_IN - KVB_TILE)
GATE_UNIT0 = 1

TQ_B = 1024
TM_MERGE = 512
TM_FFN = 512
TF_FFN = 1024


def _alibi_slopes():
    i = np.arange(1, N_ATTN_HEADS + 1, dtype=np.float32)
    return (2.0 ** (-8.0 * i / N_ATTN_HEADS)).astype(np.float32)


def _rms_scale(a, gain):
    ms = jnp.mean(a * a, axis=-1, keepdims=True)
    return a * lax.rsqrt(ms + EPS) * gain


def _dot(a, b):
    return jnp.dot(a, b, preferred_element_type=F32)


def _dot_nt(a, b):
    return lax.dot_general(a, b, (((1,), (1,)), ((), ())), preferred_element_type=F32)


def _inproj_kernel(x_ref, g1_ref, w_ref, wtail_ref, qkg_ref, heads_ref, gate_ref, h_scr, acc_scr, tmp_scr, *, n_steps):
    s = pl.program_id(0)
    p = s % N_TILES_IN
    t = jnp.maximum(s - 1, 0) % N_TILES_IN - 1
    has_matmul = s < n_steps - 1
    tm = x_ref.shape[0]

    def matmul_tile():
        acc = _dot(h_scr[...], w_ref[...])
        for c in range(HEADS_PER_TILE):
            acc_scr[c] = acc[:, c * HEAD_DIM:(c + 1) * HEAD_DIM]

    def matmul_tail():
        acc = _dot(h_scr[...], wtail_ref[...])
        for c in range(TAIL_COLS // HEAD_DIM):
            acc_scr[c] = acc[:, c * HEAD_DIM:(c + 1) * HEAD_DIM]

    def write_head(c, d, gain):
        def emit(a, res, rows):
            if gain is not None:
                a = _rms_scale(a, gain)
            heads_ref[c, res * rows:(res + 1) * rows, :] = a.astype(BF16)

        if d == 1:
            emit(acc_scr[c], 0, tm)
        elif d == 4:
            for res in range(4):
                emit(acc_scr[c, pl.ds(res, tm // 4, stride=4), :], res, tm // 4)
        else:
            tmp = tmp_scr.at[c % 2]
            for r4 in range(4):
                tmp[r4 * (tm // 4):(r4 + 1) * (tm // 4), :] = acc_scr[c, pl.ds(r4, tm // 4, stride=4), :]
            for r4 in range(4):
                for q in range(4):
                    emit(tmp[pl.ds(r4 * (tm // 4) + q, tm // 16, stride=4), :], r4 + 4 * q, tm // 16)

    def write_gates(c0, unit):
        for c in range(GATE_W // HEAD_DIM):
            z = acc_scr[c0 + c]
            gate_ref[unit, :, c * HEAD_DIM:(c + 1) * HEAD_DIM] = (0.5 + 0.5 * jnp.tanh(0.5 * z)).astype(BF16)

    def write_gate_tile():
        write_gates(0, 0)
        write_gates(GATE_W // HEAD_DIM, 1)

    def write_gate_tail():
        write_gates(0, 0)
        gate_ref[1] = jnp.zeros(gate_ref.shape[1:], BF16)

    @pl.when(has_matmul & (p == 0))
    def _():
        @pl.when(s > 0)
        def _():
            write_gate_tile()
        rows = 256
        for c in range(tm // rows):
            x = x_ref[c * rows:(c + 1) * rows, :]
            h_scr[c * rows:(c + 1) * rows, :] = _rms_scale(x, g1_ref[...]).astype(BF16)
        matmul_tail()

    @pl.when(s == n_steps - 1)
    def _():
        write_gate_tile()

    @pl.when(has_matmul & (p == 1))
    def _():
        write_gate_tail()
        matmul_tile()

    mid_row = has_matmul & (p > 1)
    role = t // N_GROUPS
    group = t % N_GROUPS
    is_a = mid_row & (t < A_TILES)

    for g, d in enumerate(DILATIONS):
        @pl.when(is_a & (group == g) & (role < 2))
        def _(d=d):
            gain = jnp.where(role == 0, qkg_ref[0:1, :], qkg_ref[1:2, :])
            for c in range(HEADS_PER_TILE):
                write_head(c, d, gain)
            matmul_tile()

        @pl.when(is_a & (group == g) & (role == 2))
        def _(d=d):
            for c in range(HEADS_PER_TILE):
                write_head(c, d, None)
            matmul_tile()

    @pl.when(mid_row & (t >= QB_TILE0) & (t < KVB_TILE))
    def _():
        for c in range(HEADS_PER_TILE):
            write_head(c, 1, qkg_ref[2:3, :])
        matmul_tile()

    @pl.when(mid_row & (t == KVB_TILE))
    def _():
        for c in range(B_KV_HEADS):
            write_head(c, 1, qkg_ref[3:4, :])
        for c in range(B_KV_HEADS, HEADS_PER_TILE):
            write_head(c, 1, None)
        write_gate_tile()
        matmul_tile()

    @pl.when(mid_row & (t > KVB_TILE))
    def _():
        write_gate_tile()
        matmul_tile()


def _in_projection(x2, g1, w, w_tail, qk_gains):
    m = x2.shape[0]
    tm = TM_IN
    n_rows = m // tm
    n_steps = n_rows * N_TILES_IN + 1

    def row_of(s):
        return jnp.maximum(s - 1, 0) // N_TILES_IN

    def tile_of(s):
        return jnp.maximum(s - 1, 0) % N_TILES_IN - 1

    out_shapes = [
        jax.ShapeDtypeStruct((N_HEADS_OUT, n_rows, tm, HEAD_DIM), BF16),
        jax.ShapeDtypeStruct((GATE_UNITS, m, GATE_W), BF16),
    ]
    out_specs = [
        pl.BlockSpec((HEADS_PER_TILE, None, tm, HEAD_DIM),
                     lambda s: (jnp.clip(tile_of(s), 0, KVB_TILE), row_of(s), 0, 0)),
        pl.BlockSpec((2, tm, GATE_W),
                     lambda s: (jnp.where(tile_of(s) >= KVB_TILE, tile_of(s) - KVB_TILE, TAIL_TILE - KVB_TILE),
                                row_of(s), 0)),
    ]
    in_specs = [
        pl.BlockSpec((tm, D_MODEL), lambda s: (jnp.minimum(s // N_TILES_IN, n_rows - 1), 0)),
        pl.BlockSpec((1, D_MODEL), lambda s: (0, 0)),
        pl.BlockSpec((D_MODEL, TN_IN),
                     lambda s: (0, jnp.where(s < n_steps - 1, jnp.maximum(s % N_TILES_IN - 1, 0),
                                             TAIL_TILE - 1))),
        pl.BlockSpec((D_MODEL, TAIL_COLS), lambda s: (0, 0), pipeline_mode=pl.Buffered(1)),
        pl.BlockSpec((4, HEAD_DIM), lambda s: (0, 0)),
    ]
    return pl.pallas_call(
        functools.partial(_inproj_kernel, n_steps=n_steps),
        grid=(n_steps,),
        in_specs=in_specs,
        out_specs=out_specs,
        out_shape=out_shapes,
        scratch_shapes=[pltpu.VMEM((tm, D_MODEL), BF16),
                        pltpu.VMEM((HEADS_PER_TILE, tm, HEAD_DIM), F32),
                        pltpu.VMEM((2, tm, HEAD_DIM), F32)],
        compiler_params=pltpu.CompilerParams(
            dimension_semantics=("arbitrary",), vmem_limit_bytes=VMEM_LIMIT),
        name="in_projection",
    )(x2, g1, w, w_tail, qk_gains)


def _band_bias(slope_times_stride, max_back):
    qi = lax.broadcasted_iota(jnp.int32, (BLOCK, 2 * BLOCK), 0) + BLOCK
    ki = lax.broadcasted_iota(jnp.int32, (BLOCK, 2 * BLOCK), 1)
    rel = qi - ki
    valid = (rel >= 0) & (rel <= max_back)
    bias = jnp.where(valid, -slope_times_stride * rel.astype(F32), NEG_INF)
    return bias, ki


def _with_ones(vw):
    return jnp.concatenate([vw, jnp.ones(vw.shape, vw.dtype)], axis=-1)


def _attn_units(units):
    scores = [_dot_nt(q, kw) + bias for q, kw, _, bias in units]
    maxes = [jnp.max(s, axis=-1, keepdims=True) for s in scores]
    probs = [jnp.exp2(s - m).astype(BF16) for s, m in zip(scores, maxes)]
    outs = [_dot(p, _with_ones(u[2])) for p, u in zip(probs, units)]
    results = []
    for o_l, m in zip(outs, maxes):
        o, l = o_l[:, :HEAD_DIM], o_l[:, HEAD_DIM:]
        results.append((o / l, m + jnp.log2(l)))
    return results


UNITS_PER_BATCH = 16
TILES_PER_SUPER = SUPER // TM_IN


def _mixer_a_kernel(slopes_ref, q0_ref, k0c_ref, v0c_ref, q1_ref, k1c_ref, v1c_ref, q2_ref, k2c_ref, v2c_ref,
                    w1_ref, w2_ref, o_ref, w1_out_ref, w2_out_ref,
                    o1_scr, l1_scr, o2_scr, l2_scr, o4_scr, l4_scr,
                    k0p_ref, v0p_ref, k1p_ref, v1p_ref, k2p_ref, v2p_ref):
    w1_out_ref[...] = w1_ref[...].astype(BF16)
    w2_out_ref[...] = w2_ref[...].astype(BF16)

    h = pl.program_id(1)
    first = pl.program_id(2) == 0
    prev_refs = (k0p_ref, v0p_ref, k1p_ref, v1p_ref, k2p_ref, v2p_ref)

    @pl.when(first)
    def _():
        for ref in prev_refs:
            ref[...] = jnp.zeros(ref.shape, ref.dtype)

    biases = []
    for g, d in enumerate(DILATIONS):
        bias, ki = _band_bias(slopes_ref[g * A_HPG + h] * float(d), A_MAX_BACK)
        bias_head = jnp.where(first, jnp.where(ki < BLOCK, NEG_INF, bias), bias)
        biases.append((bias, bias_head))

    def cat(*parts):
        return jnp.concatenate(parts, axis=0)

    def block16(ref, r):
        return cat(ref[0, r], ref[1, r])

    def body2(it, carry):
        rs = [it * UNITS_PER_BATCH + u for u in range(UNITS_PER_BATCH)]
        units = [(block16(q2_ref, r), cat(block16(k2p_ref, r), block16(k2c_ref, r)),
                  cat(block16(v2p_ref, r), block16(v2c_ref, r)), biases[2][1]) for r in rs]
        for r, (o, lse) in zip(rs, _attn_units(units)):
            rows = pl.ds((r % 4) * (SUPER // 4) + r // 4, BLOCK, stride=4)
            o4_scr[rows, :] = o
            l4_scr[rows, :] = lse
        return carry

    lax.fori_loop(0, 16 // UNITS_PER_BATCH, body2, 0)
    for r4 in range(4):
        seg = slice(r4 * (SUPER // 4), (r4 + 1) * (SUPER // 4))
        o2_scr[pl.ds(r4, SUPER // 4, stride=4), :] = o4_scr[seg, :]
        l2_scr[pl.ds(r4, SUPER // 4, stride=4), :] = l4_scr[seg, :]

    def block4(ref, r, blk):
        half = (blk % 2) * BLOCK
        return ref[blk // 2, r, half:half + BLOCK]

    def window4(cur_ref, prev_ref, r, blk):
        prev = prev_ref[r] if blk == 0 else block4(cur_ref, r, blk - 1)
        return cat(prev, block4(cur_ref, r, blk))

    def body1(it, carry):
        per = UNITS_PER_BATCH // 4
        rb = [(it * per + u // 4, u % 4) for u in range(UNITS_PER_BATCH)]
        units = [(block4(q1_ref, r, blk), window4(k1c_ref, k1p_ref, r, blk), window4(v1c_ref, v1p_ref, r, blk),
                  biases[1][1] if blk == 0 else biases[1][0]) for r, blk in rb]
        for (r, blk), (o, lse) in zip(rb, _attn_units(units)):
            rows = pl.ds(blk * BLOCK * 4 + r, BLOCK, stride=4)
            o1_scr[rows, :] = o
            l1_scr[rows, :] = lse
        return carry

    lax.fori_loop(0, 16 // UNITS_PER_BATCH, body1, 0)

    def window1(cur_ref, prev_ref, blk):
        if blk == 0:
            return cat(prev_ref[...], cur_ref[0:BLOCK])
        return cur_ref[(blk - 1) * BLOCK:(blk + 1) * BLOCK]

    for b0 in range(0, SUPER // BLOCK, UNITS_PER_BATCH):
        blks = list(range(b0, b0 + UNITS_PER_BATCH))
        units = [(q0_ref[blk * BLOCK:(blk + 1) * BLOCK, :], window1(k0c_ref, k0p_ref, blk),
                  window1(v0c_ref, v0p_ref, blk), biases[0][1] if blk == 0 else biases[0][0])
                 for blk in blks]
        for blk, (o0, lse0) in zip(blks, _attn_units(units)):
            rows = slice(blk * BLOCK, (blk + 1) * BLOCK)
            lse1 = l1_scr[rows, :]
            lse2 = l2_scr[rows, :]
            mx = jnp.maximum(jnp.maximum(lse0, lse1), lse2)
            w0 = jnp.exp2(lse0 - mx)
            w1 = jnp.exp2(lse1 - mx)
            w2 = jnp.exp2(lse2 - mx)
            o = (w0 * o0 + w1 * o1_scr[rows, :] + w2 * o2_scr[rows, :]) / (w0 + w1 + w2)
            o_ref[rows, :] = o.astype(BF16)

    k0p_ref[...] = k0c_ref[SUPER - BLOCK:SUPER]
    v0p_ref[...] = v0c_ref[SUPER - BLOCK:SUPER]
    k1p_ref[...] = k1c_ref[TILES_PER_SUPER - 1, :, TM_IN // 4 - BLOCK:TM_IN // 4]
    v1p_ref[...] = v1c_ref[TILES_PER_SUPER - 1, :, TM_IN // 4 - BLOCK:TM_IN // 4]
    k2p_ref[...] = k2c_ref[...]
    v2p_ref[...] = v2c_ref[...]


def _mixer_a(heads, slopes_a, w_ff1, w_ff2, bsz, seq):
    m = bsz * seq
    nsb = seq // SUPER
    n_rows = m // TM_IN
    grid = (bsz, A_HPG, nsb)
    n_steps = bsz * nsb * A_HPG
    assert w_ff1.shape[0] % (16 * n_steps) == 0 and w_ff2.shape[0] % (16 * n_steps) == 0

    def weight_slice(w):
        return pl.BlockSpec((w.shape[0] // n_steps, w.shape[1]), lambda b, h, s: ((b * A_HPG + h) * nsb + s, 0))
    nat = heads.reshape(N_HEADS_OUT, m, HEAD_DIM)
    by4 = heads.reshape(N_HEADS_OUT, n_rows, 4, TM_IN // 4, HEAD_DIM)
    by16 = heads.reshape(N_HEADS_OUT, n_rows, 16, TM_IN // 16, HEAD_DIM)

    def head(role, g, h):
        return (role * N_GROUPS + g) * A_HPG + h

    def nat_cur(role):
        return pl.BlockSpec((None, SUPER, HEAD_DIM), lambda b, h, s: (head(role, 0, h), b * nsb + s, 0))

    def dil_cur(role, g, d):
        return pl.BlockSpec((None, TILES_PER_SUPER, d, TM_IN // d, HEAD_DIM),
                            lambda b, h, s: (head(role, g, h), b * nsb + s, 0, 0, 0))

    in_specs = [pl.BlockSpec(memory_space=pltpu.SMEM),
                nat_cur(0), nat_cur(1), nat_cur(2),
                dil_cur(0, 1, 4), dil_cur(1, 1, 4), dil_cur(2, 1, 4),
                dil_cur(0, 2, 16), dil_cur(1, 2, 16), dil_cur(2, 2, 16),
                weight_slice(w_ff1), weight_slice(w_ff2)]
    prev_shapes = [(BLOCK, HEAD_DIM), (4, BLOCK, HEAD_DIM), (TILES_PER_SUPER, 16, TM_IN // 16, HEAD_DIM)]
    return pl.pallas_call(
        _mixer_a_kernel,
        grid=grid,
        in_specs=in_specs,
        out_specs=[pl.BlockSpec((None, SUPER, HEAD_DIM), lambda b, h, s: (h, b * nsb + s, 0)),
                   weight_slice(w_ff1), weight_slice(w_ff2)],
        out_shape=[jax.ShapeDtypeStruct((A_HPG, m, HEAD_DIM), BF16),
                   jax.ShapeDtypeStruct(w_ff1.shape, BF16), jax.ShapeDtypeStruct(w_ff2.shape, BF16)],
        scratch_shapes=([pltpu.VMEM((SUPER, HEAD_DIM), F32) for _ in range(6)]
                        + [pltpu.VMEM(shape, BF16) for shape in prev_shapes for _ in range(2)]),
        compiler_params=pltpu.CompilerParams(
            dimension_semantics=("arbitrary", "arbitrary", "arbitrary"), vmem_limit_bytes=VMEM_LIMIT),
        name="mixer_a",
    )(slopes_a, nat, nat, nat, by4, by4, by4, by16, by16, by16, w_ff1, w_ff2)


def _mixer_b_kernel(slopes_ref, sinks_ref, q_ref, kc_ref, kp_ref, vc_ref, vp_ref, *rest, tiles_per_seq):
    n_w = (len(rest) - 1) // 2
    w_refs, o_ref, w_out_refs = rest[:n_w], rest[n_w], rest[n_w + 1:]
    for w_ref, w_out_ref in zip(w_refs, w_out_refs):
        w_out_ref[...] = w_ref[...].astype(BF16)

    first = (pl.program_id(0) % tiles_per_seq) == 0
    kvh = pl.program_id(1)
    tq = kc_ref.shape[0]

    biases, biases_head = [], []
    for hh in range(B_GROUP):
        bias, ki = _band_bias(slopes_ref[kvh * B_GROUP + hh], B_MAX_BACK)
        biases.append(bias)
        biases_head.append(jnp.where(first, jnp.where(ki < BLOCK, NEG_INF, bias), bias))

    for blk in range(tq // BLOCK):
        if blk == 0:
            kw = jnp.concatenate([kp_ref[...], kc_ref[0:BLOCK]], axis=0)
            vw = jnp.concatenate([vp_ref[...], vc_ref[0:BLOCK]], axis=0)
        else:
            kw = kc_ref[(blk - 1) * BLOCK:(blk + 1) * BLOCK]
            vw = vc_ref[(blk - 1) * BLOCK:(blk + 1) * BLOCK]
        q = q_ref[:, blk * BLOCK:(blk + 1) * BLOCK, :].reshape(B_GROUP * BLOCK, HEAD_DIM)
        s_all = _dot_nt(q, kw)
        ps, sink_terms = [], []
        for hh in range(B_GROUP):
            s = s_all[hh * BLOCK:(hh + 1) * BLOCK] + (biases_head[hh] if blk == 0 else biases[hh])
            sink = sinks_ref[kvh * B_GROUP + hh]
            mrow = jnp.maximum(jnp.max(s, axis=-1, keepdims=True), sink)
            ps.append(jnp.exp2(s - mrow).astype(BF16))
            sink_terms.append(jnp.exp2(sink - mrow))
        o_all = _dot(jnp.concatenate(ps, axis=0), _with_ones(vw))
        for hh in range(B_GROUP):
            rows = slice(hh * BLOCK, (hh + 1) * BLOCK)
            o = o_all[rows, :HEAD_DIM] / (o_all[rows, HEAD_DIM:] + sink_terms[hh])
            o_ref[hh, blk * BLOCK:(blk + 1) * BLOCK, :] = o.astype(BF16)


def _mixer_b(heads, slopes_b, sinks, weights, bsz, seq):
    m = bsz * seq
    tq = TQ_B
    per = tq // BLOCK
    grid = (m // tq, B_KV_HEADS)
    n_steps = grid[0] * grid[1]
    assert all(w.shape[0] % (16 * n_steps) == 0 for w in weights)
    nat = heads.reshape(N_HEADS_OUT, m, HEAD_DIM)

    def weight_slice(w):
        return pl.BlockSpec((w.shape[0] // n_steps, w.shape[1]), lambda i, c: (i * B_KV_HEADS + c, 0))

    def cur(head0):
        return pl.BlockSpec((None, tq, HEAD_DIM), lambda i, c: (head0 + c, i, 0))

    def prev(head0):
        return pl.BlockSpec((None, BLOCK, HEAD_DIM), lambda i, c: (head0 + c, jnp.maximum(i * per - 1, 0), 0))

    smem = pl.BlockSpec(memory_space=pltpu.SMEM)
    return pl.pallas_call(
        functools.partial(_mixer_b_kernel, tiles_per_seq=seq // tq),
        grid=grid,
        in_specs=[smem, smem,
                  pl.BlockSpec((B_GROUP, tq, HEAD_DIM), lambda i, c: (QB_HEAD0 // B_GROUP + c, i, 0)),
                  cur(KB_HEAD0), prev(KB_HEAD0), cur(VB_HEAD0), prev(VB_HEAD0)]
                 + [weight_slice(w) for w in weights],
        out_specs=[pl.BlockSpec((B_GROUP, tq, HEAD_DIM), lambda i, c: (c, i, 0))]
                  + [weight_slice(w) for w in weights],
        out_shape=[jax.ShapeDtypeStruct((B_Q_HEADS, m, HEAD_DIM), BF16)]
                  + [jax.ShapeDtypeStruct(w.shape, BF16) for w in weights],
        compiler_params=pltpu.CompilerParams(
            dimension_semantics=("arbitrary", "arbitrary"), vmem_limit_bytes=VMEM_LIMIT),
        name="mixer_b",
    )(slopes_b, sinks, nat, nat, nat, nat, nat, *weights)


def _merge_kernel(oa_ref, ob_ref, gate_ref, wa_ref, wb_ref, out_ref):
    oa = jnp.concatenate([oa_ref[hh] for hh in range(A_HPG)], axis=-1)
    ob = jnp.concatenate([ob_ref[hh] for hh in range(B_Q_HEADS)], axis=-1)
    per = D_MODEL // GATE_W
    ga = jnp.concatenate([gate_ref[GATE_UNIT0 + c] for c in range(per)], axis=-1).astype(F32)
    gb = jnp.concatenate([gate_ref[GATE_UNIT0 + per + c] for c in range(per)], axis=-1).astype(F32)
    merged = ga * _dot(oa, wa_ref[...]) + gb * _dot(ob, wb_ref[...])
    out_ref[...] = merged.astype(BF16)


def _merge(oa, ob, gates, wa, wb):
    m = gates.shape[1]
    tm = TM_MERGE
    resident = dict(pipeline_mode=pl.Buffered(1))
    return pl.pallas_call(
        _merge_kernel,
        grid=(m // tm,),
        in_specs=[pl.BlockSpec((A_HPG, tm, HEAD_DIM), lambda i: (0, i, 0)),
                  pl.BlockSpec((B_Q_HEADS, tm, HEAD_DIM), lambda i: (0, i, 0)),
                  pl.BlockSpec((GATE_UNITS, tm, GATE_W), lambda i: (0, i, 0)),
                  pl.BlockSpec(wa.shape, lambda i: (0, 0), **resident),
                  pl.BlockSpec(wb.shape, lambda i: (0, 0), **resident)],
        out_specs=pl.BlockSpec((tm, D_MODEL), lambda i: (i, 0)),
        out_shape=jax.ShapeDtypeStruct((m, D_MODEL), BF16),
        compiler_params=pltpu.CompilerParams(
            dimension_semantics=("arbitrary",), vmem_limit_bytes=VMEM_LIMIT),
        name="gated_merge",
    )(oa, ob, gates, wa, wb)


def _ffn_kernel(x_ref, mg_ref, wo_ref, g2_ref, w1_ref, w2_ref, out_ref, h2_scr):
    @pl.when(pl.program_id(1) == 0)
    def _():
        x1 = x_ref[...] + _dot(mg_ref[...], wo_ref[...])
        out_ref[...] = x1
        h2_scr[...] = _rms_scale(x1, g2_ref[...]).astype(BF16)

    z = _dot(h2_scr[...], w1_ref[...])
    act = jnp.square(jnp.maximum(z, 0.0)).astype(BF16)
    out_ref[...] += _dot(act, w2_ref[...])


def _ffn(x2, merged, wo, g2, w1, w2):
    m = x2.shape[0]
    tm, tf = TM_FFN, TF_FFN
    return pl.pallas_call(
        _ffn_kernel,
        grid=(m // tm, D_FF // tf),
        in_specs=[pl.BlockSpec((tm, D_MODEL), lambda i, f: (i, 0)),
                  pl.BlockSpec((tm, D_MODEL), lambda i, f: (i, 0)),
                  pl.BlockSpec(wo.shape, lambda i, f: (0, 0), pipeline_mode=pl.Buffered(1)),
                  pl.BlockSpec((1, D_MODEL), lambda i, f: (0, 0)),
                  pl.BlockSpec((D_MODEL, tf), lambda i, f: (0, f)),
                  pl.BlockSpec((tf, D_MODEL), lambda i, f: (f, 0))],
        out_specs=pl.BlockSpec((tm, D_MODEL), lambda i, f: (i, 0)),
        out_shape=jax.ShapeDtypeStruct((m, D_MODEL), F32),
        scratch_shapes=[pltpu.VMEM((tm, D_MODEL), BF16)],
        compiler_params=pltpu.CompilerParams(
            dimension_semantics=("arbitrary", "arbitrary"), vmem_limit_bytes=VMEM_LIMIT),
        name="out_proj_ffn",
    )(x2, merged, wo, g2, w1, w2)


def kernel(x, norm1_g, w_in, q_norm_a, k_norm_a, q_norm_b, k_norm_b, sinks_b, w_branch_a, w_branch_b,
           w_out, norm2_g, w_ff1, w_ff2):
    bsz, seq, d_model = x.shape
    depth = w_in.shape[0]
    assert d_model == D_MODEL and w_in.shape[2] == IN_COLS
    assert seq % SUPER == 0 and seq % TQ_B == 0 and SUPER % TM_IN == 0
    m = bsz * seq
    slopes = _alibi_slopes() * np.float32(LOG2E)
    slopes_b = jnp.asarray(slopes[:B_Q_HEADS])
    slopes_a = jnp.asarray(slopes[B_Q_HEADS:])
    scale = HEAD_DIM ** -0.5 * LOG2E

    x2 = x.reshape(m, D_MODEL)
    for l in range(depth):
        qk_gains = jnp.stack([q_norm_a[l] * scale, k_norm_a[l], q_norm_b[l] * scale, k_norm_b[l]])
        w_tail = w_in[l][:, TAIL_TILE * TN_IN:].astype(BF16)
        heads, gates = _in_projection(x2, norm1_g[l].reshape(1, D_MODEL), w_in[l].astype(BF16), w_tail, qk_gains)
        oa, w1_bf16, w2_bf16 = _mixer_a(heads, slopes_a, w_ff1[l], w_ff2[l], bsz, seq)
        ob, wa_bf16, wb_bf16, wo_bf16 = _mixer_b(heads, slopes_b, sinks_b[l] * LOG2E,
                                                 (w_branch_a[l], w_branch_b[l], w_out[l]), bsz, seq)
        merged = _merge(oa, ob, gates, wa_bf16, wb_bf16)
        x2 = _ffn(x2, merged, wo_bf16, norm2_g[l].reshape(1, D_MODEL),
                  w1_bf16, w2_bf16)
    return x2.reshape(bsz, seq, D_MODEL)
```

```python
import functools

import numpy as np
import jax
import jax.numpy as jnp
from jax import lax
from jax.experimental import pallas as pl
from jax.experimental.pallas import tpu as pltpu

D_MODEL = 2048
HEAD_DIM = 128
BLOCK = 128
EPS = 1e-6
DILATIONS = (1, 4, 16)
A_MAX_BACK = 128
N_GROUPS = 3
A_HPG = 8
A_WIDTH = N_GROUPS * A_HPG * HEAD_DIM
B_Q_HEADS = 16
B_KV_HEADS = 2
B_GROUP = B_Q_HEADS // B_KV_HEADS
B_MAX_BACK = 127
D_FF = 4 * D_MODEL
N_ATTN_HEADS = B_Q_HEADS + N_GROUPS * A_HPG
IN_COLS = 3 * A_WIDTH + B_Q_HEADS * HEAD_DIM + 2 * B_KV_HEADS * HEAD_DIM + 2 * D_MODEL

SUPER = BLOCK * DILATIONS[-1]
NEG_INF = float("-inf")
LOG2E = 1.4426950408889634
BF16 = jnp.bfloat16
F32 = jnp.float32

VMEM_LIMIT = 56 * 1024 * 1024

TM_IN = 1024
TN_IN = 1024
HEADS_PER_TILE = TN_IN // HEAD_DIM
A_TILES = 3 * N_GROUPS
QB_TILE0 = A_TILES
QB_TILES = B_Q_HEADS // HEADS_PER_TILE
KVB_TILE = QB_TILE0 + QB_TILES
N_TILES_IN = -(-IN_COLS // TN_IN)
TAIL_TILE = N_TILES_IN - 1
TAIL_COLS = IN_COLS - TAIL_TILE * TN_IN
N_HEAD_TILES = KVB_TILE + 1
N_HEADS_OUT = N_HEAD_TILES * HEADS_PER_TILE
QB_HEAD0 = QB_TILE0 * HEADS_PER_TILE
KB_HEAD0 = KVB_TILE * HEADS_PER_TILE
VB_HEAD0 = KB_HEAD0 + B_KV_HEADS
GATE_W = TN_IN // 2
GATE_UNITS = 2 * (N_TILES_IN - KVB_TILE)
GATE_UNIT0 = 1

TQ_B = 2048
TM_MERGE = 512
TM_FFN = 512
TF_FFN = 1024


def _alibi_slopes():
    i = np.arange(1, N_ATTN_HEADS + 1, dtype=np.float32)
    return (2.0 ** (-8.0 * i / N_ATTN_HEADS)).astype(np.float32)


def _rms_scale(a, gain):
    ms = jnp.mean(a * a, axis=-1, keepdims=True)
    return a * lax.rsqrt(ms + EPS) * gain


def _dot(a, b):
    return jnp.dot(a, b, preferred_element_type=F32)


def _dot_nt(a, b):
    return lax.dot_general(a, b, (((1,), (1,)), ((), ())), preferred_element_type=F32)


def _inproj_kernel(x_ref, g1_ref, w_ref, wtail_ref, qkg_ref, heads_ref, gate_ref, h_scr, acc_scr, tmp_scr, *, n_steps):
    s = pl.program_id(0)
    p = s % N_TILES_IN
    t = jnp.maximum(s - 1, 0) % N_TILES_IN - 1
    has_matmul = s < n_steps - 1
    tm = x_ref.shape[0]

    def matmul_tile():
        acc = _dot(h_scr[...], w_ref[...])
        for c in range(HEADS_PER_TILE):
            acc_scr[c] = acc[:, c * HEAD_DIM:(c + 1) * HEAD_DIM]

    def matmul_tail():
        acc = _dot(h_scr[...], wtail_ref[...])
        for c in range(TAIL_COLS // HEAD_DIM):
            acc_scr[c] = acc[:, c * HEAD_DIM:(c + 1) * HEAD_DIM]

    def write_head(c, d, gain):
        def emit(a, res, rows):
            if gain is not None:
                a = _rms_scale(a, gain)
            heads_ref[c, res * rows:(res + 1) * rows, :] = a.astype(BF16)

        if d == 1:
            emit(acc_scr[c], 0, tm)
        elif d == 4:
            for res in range(4):
                emit(acc_scr[c, pl.ds(res, tm // 4, stride=4), :], res, tm // 4)
        else:
            tmp = tmp_scr.at[c % 2]
            for r4 in range(4):
                tmp[r4 * (tm // 4):(r4 + 1) * (tm // 4), :] = acc_scr[c, pl.ds(r4, tm // 4, stride=4), :]
            for r4 in range(4):
                for q in range(4):
                    emit(tmp[pl.ds(r4 * (tm // 4) + q, tm // 16, stride=4), :], r4 + 4 * q, tm // 16)

    def write_gates(c0, unit):
        for c in range(GATE_W // HEAD_DIM):
            z = acc_scr[c0 + c]
            gate_ref[unit, :, c * HEAD_DIM:(c + 1) * HEAD_DIM] = (0.5 + 0.5 * jnp.tanh(0.5 * z)).astype(BF16)

    def write_gate_tile():
        write_gates(0, 0)
        write_gates(GATE_W // HEAD_DIM, 1)

    def write_gate_tail():
        write_gates(0, 0)
        gate_ref[1] = jnp.zeros(gate_ref.shape[1:], BF16)

    @pl.when(has_matmul & (p == 0))
    def _():
        @pl.when(s > 0)
        def _():
            write_gate_tile()
        rows = 256
        for c in range(tm // rows):
            x = x_ref[c * rows:(c + 1) * rows, :]
            h_scr[c * rows:(c + 1) * rows, :] = _rms_scale(x, g1_ref[...]).astype(BF16)
        matmul_tail()

    @pl.when(s == n_steps - 1)
    def _():
        write_gate_tile()

    @pl.when(has_matmul & (p == 1))
    def _():
        write_gate_tail()
        matmul_tile()

    mid_row = has_matmul & (p > 1)
    role = t // N_GROUPS
    group = t % N_GROUPS
    is_a = mid_row & (t < A_TILES)

    for g, d in enumerate(DILATIONS):
        @pl.when(is_a & (group == g) & (role < 2))
        def _(d=d):
            gain = jnp.where(role == 0, qkg_ref[0:1, :], qkg_ref[1:2, :])
            for c in range(HEADS_PER_TILE):
                write_head(c, d, gain)
            matmul_tile()

        @pl.when(is_a & (group == g) & (role == 2))
        def _(d=d):
            for c in range(HEADS_PER_TILE):
                write_head(c, d, None)
            matmul_tile()

    @pl.when(mid_row & (t >= QB_TILE0) & (t < KVB_TILE))
    def _():
        for c in range(HEADS_PER_TILE):
            write_head(c, 1, qkg_ref[2:3, :])
        matmul_tile()

    @pl.when(mid_row & (t == KVB_TILE))
    def _():
        for c in range(B_KV_HEADS):
            write_head(c, 1, qkg_ref[3:4, :])
        for c in range(B_KV_HEADS, HEADS_PER_TILE):
            write_head(c, 1, None)
        write_gate_tile()
        matmul_tile()

    @pl.when(mid_row & (t > KVB_TILE))
    def _():
        write_gate_tile()
        matmul_tile()


def _in_projection(x2, g1, w, w_tail, qk_gains):
    m = x2.shape[0]
    tm = TM_IN
    n_rows = m // tm
    n_steps = n_rows * N_TILES_IN + 1

    def row_of(s):
        return jnp.maximum(s - 1, 0) // N_TILES_IN

    def tile_of(s):
        return jnp.maximum(s - 1, 0) % N_TILES_IN - 1

    out_shapes = [
        jax.ShapeDtypeStruct((N_HEADS_OUT, n_rows, tm, HEAD_DIM), BF16),
        jax.ShapeDtypeStruct((GATE_UNITS, m, GATE_W), BF16),
    ]
    out_specs = [
        pl.BlockSpec((HEADS_PER_TILE, None, tm, HEAD_DIM),
                     lambda s: (jnp.clip(tile_of(s), 0, KVB_TILE), row_of(s), 0, 0)),
        pl.BlockSpec((2, tm, GATE_W),
                     lambda s: (jnp.where(tile_of(s) >= KVB_TILE, tile_of(s) - KVB_TILE, TAIL_TILE - KVB_TILE),
                                row_of(s), 0)),
    ]
    in_specs = [
        pl.BlockSpec((tm, D_MODEL), lambda s: (jnp.minimum(s // N_TILES_IN, n_rows - 1), 0)),
        pl.BlockSpec((1, D_MODEL), lambda s: (0, 0)),
        pl.BlockSpec((D_MODEL, TN_IN),
                     lambda s: (0, jnp.where(s < n_steps - 1, jnp.maximum(s % N_TILES_IN - 1, 0),
                                             TAIL_TILE - 1))),
        pl.BlockSpec((D_MODEL, TAIL_COLS), lambda s: (0, 0), pipeline_mode=pl.Buffered(1)),
        pl.BlockSpec((4, HEAD_DIM), lambda s: (0, 0)),
    ]
    return pl.pallas_call(
        functools.partial(_inproj_kernel, n_steps=n_steps),
        grid=(n_steps,),
        in_specs=in_specs,
        out_specs=out_specs,
        out_shape=out_shapes,
        scratch_shapes=[pltpu.VMEM((tm, D_MODEL), BF16),
                        pltpu.VMEM((HEADS_PER_TILE, tm, HEAD_DIM), F32),
                        pltpu.VMEM((2, tm, HEAD_DIM), F32)],
        compiler_params=pltpu.CompilerParams(
            dimension_semantics=("arbitrary",), vmem_limit_bytes=VMEM_LIMIT),
        name="in_projection",
    )(x2, g1, w, w_tail, qk_gains)


def _band_bias(slope_times_stride, max_back):
    qi = lax.broadcasted_iota(jnp.int32, (BLOCK, 2 * BLOCK), 0) + BLOCK
    ki = lax.broadcasted_iota(jnp.int32, (BLOCK, 2 * BLOCK), 1)
    rel = qi - ki
    valid = (rel >= 0) & (rel <= max_back)
    bias = jnp.where(valid, -slope_times_stride * rel.astype(F32), NEG_INF)
    return bias, ki


def _with_ones(vw):
    return jnp.concatenate([vw, jnp.ones(vw.shape, vw.dtype)], axis=-1)


def _attn_units(units):
    scores = [_dot_nt(q, kw) + bias for q, kw, _, bias in units]
    maxes = [jnp.max(s, axis=-1, keepdims=True) for s in scores]
    probs = [jnp.exp2(s - m).astype(BF16) for s, m in zip(scores, maxes)]
    outs = [_dot(p, _with_ones(u[2])) for p, u in zip(probs, units)]
    results = []
    for o_l, m in zip(outs, maxes):
        o, l = o_l[:, :HEAD_DIM], o_l[:, HEAD_DIM:]
        results.append((o / l, m + jnp.log2(l)))
    return results


UNITS_PER_BATCH = 16
TILES_PER_SUPER = SUPER // TM_IN


def _mixer_a_kernel(slopes_ref, q0_ref, k0c_ref, v0c_ref, q1_ref, k1c_ref, v1c_ref, q2_ref, k2c_ref, v2c_ref,
                    w1_ref, w2_ref, o_ref, w1_out_ref, w2_out_ref,
                    o1_scr, l1_scr, o2_scr, l2_scr, o4_scr, l4_scr,
                    k0p_ref, v0p_ref, k1p_ref, v1p_ref, k2p_ref, v2p_ref):
    w1_out_ref[...] = w1_ref[...].astype(BF16)
    w2_out_ref[...] = w2_ref[...].astype(BF16)

    h = pl.program_id(1)
    first = pl.program_id(2) == 0
    prev_refs = (k0p_ref, v0p_ref, k1p_ref, v1p_ref, k2p_ref, v2p_ref)

    @pl.when(first)
    def _():
        for ref in prev_refs:
            ref[...] = jnp.zeros(ref.shape, ref.dtype)

    biases = []
    for g, d in enumerate(DILATIONS):
        bias, ki = _band_bias(slopes_ref[g * A_HPG + h] * float(d), A_MAX_BACK)
        bias_head = jnp.where(first, jnp.where(ki < BLOCK, NEG_INF, bias), bias)
        biases.append((bias, bias_head))

    def cat(*parts):
        return jnp.concatenate(parts, axis=0)

    def block16(ref, r):
        return cat(ref[0, r], ref[1, r])

    def body2(it, carry):
        rs = [it * UNITS_PER_BATCH + u for u in range(UNITS_PER_BATCH)]
        units = [(block16(q2_ref, r), cat(block16(k2p_ref, r), block16(k2c_ref, r)),
                  cat(block16(v2p_ref, r), block16(v2c_ref, r)), biases[2][1]) for r in rs]
        for r, (o, lse) in zip(rs, _attn_units(units)):
            rows = pl.ds((r % 4) * (SUPER // 4) + r // 4, BLOCK, stride=4)
            o4_scr[rows, :] = o
            l4_scr[rows, :] = lse
        return carry

    lax.fori_loop(0, 16 // UNITS_PER_BATCH, body2, 0)
    for r4 in range(4):
        seg = slice(r4 * (SUPER // 4), (r4 + 1) * (SUPER // 4))
        o2_scr[pl.ds(r4, SUPER // 4, stride=4), :] = o4_scr[seg, :]
        l2_scr[pl.ds(r4, SUPER // 4, stride=4), :] = l4_scr[seg, :]

    def block4(ref, r, blk):
        half = (blk % 2) * BLOCK
        return ref[blk // 2, r, half:half + BLOCK]

    def window4(cur_ref, prev_ref, r, blk):
        prev = prev_ref[r] if blk == 0 else block4(cur_ref, r, blk - 1)
        return cat(prev, block4(cur_ref, r, blk))

    def body1(it, carry):
        per = UNITS_PER_BATCH // 4
        rb = [(it * per + u // 4, u % 4) for u in range(UNITS_PER_BATCH)]
        units = [(block4(q1_ref, r, blk), window4(k1c_ref, k1p_ref, r, blk), window4(v1c_ref, v1p_ref, r, blk),
                  biases[1][1] if blk == 0 else biases[1][0]) for r, blk in rb]
        for (r, blk), (o, lse) in zip(rb, _attn_units(units)):
            rows = pl.ds(blk * BLOCK * 4 + r, BLOCK, stride=4)
            o1_scr[rows, :] = o
            l1_scr[rows, :] = lse
        return carry

    lax.fori_loop(0, 16 // UNITS_PER_BATCH, body1, 0)

    def window1(cur_ref, prev_ref, blk):
        if blk == 0:
            return cat(prev_ref[...], cur_ref[0:BLOCK])
        return cur_ref[(blk - 1) * BLOCK:(blk + 1) * BLOCK]

    for b0 in range(0, SUPER // BLOCK, 8):
        blks = list(range(b0, b0 + 8))
        units = [(q0_ref[blk * BLOCK:(blk + 1) * BLOCK, :], window1(k0c_ref, k0p_ref, blk),
                  window1(v0c_ref, v0p_ref, blk), biases[0][1] if blk == 0 else biases[0][0])
                 for blk in blks]
        for blk, (o0, lse0) in zip(blks, _attn_units(units)):
            rows = slice(blk * BLOCK, (blk + 1) * BLOCK)
            lse1 = l1_scr[rows, :]
            lse2 = l2_scr[rows, :]
            mx = jnp.maximum(jnp.maximum(lse0, lse1), lse2)
            w0 = jnp.exp2(lse0 - mx)
            w1 = jnp.exp2(lse1 - mx)
            w2 = jnp.exp2(lse2 - mx)
            o = (w0 * o0 + w1 * o1_scr[rows, :] + w2 * o2_scr[rows, :]) / (w0 + w1 + w2)
            o_ref[rows, :] = o.astype(BF16)

    k0p_ref[...] = k0c_ref[SUPER - BLOCK:SUPER]
    v0p_ref[...] = v0c_ref[SUPER - BLOCK:SUPER]
    k1p_ref[...] = k1c_ref[TILES_PER_SUPER - 1, :, TM_IN // 4 - BLOCK:TM_IN // 4]
    v1p_ref[...] = v1c_ref[TILES_PER_SUPER - 1, :, TM_IN // 4 - BLOCK:TM_IN // 4]
    k2p_ref[...] = k2c_ref[...]
    v2p_ref[...] = v2c_ref[...]


def _mixer_a(heads, slopes_a, w_ff1, w_ff2, bsz, seq):
    m = bsz * seq
    nsb = seq // SUPER
    n_rows = m // TM_IN
    grid = (bsz, A_HPG, nsb)
    n_steps = bsz * nsb * A_HPG
    assert w_ff1.shape[0] % (16 * n_steps) == 0 and w_ff2.shape[0] % (16 * n_steps) == 0

    def weight_slice(w):
        return pl.BlockSpec((w.shape[0] // n_steps, w.shape[1]), lambda b, h, s: ((b * A_HPG + h) * nsb + s, 0))
    nat = heads.reshape(N_HEADS_OUT, m, HEAD_DIM)
    by4 = heads.reshape(N_HEADS_OUT, n_rows, 4, TM_IN // 4, HEAD_DIM)
    by16 = heads.reshape(N_HEADS_OUT, n_rows, 16, TM_IN // 16, HEAD_DIM)

    def head(role, g, h):
        return (role * N_GROUPS + g) * A_HPG + h

    def nat_cur(role):
        return pl.BlockSpec((None, SUPER, HEAD_DIM), lambda b, h, s: (head(role, 0, h), b * nsb + s, 0))

    def dil_cur(role, g, d):
        return pl.BlockSpec((None, TILES_PER_SUPER, d, TM_IN // d, HEAD_DIM),
                            lambda b, h, s: (head(role, g, h), b * nsb + s, 0, 0, 0))

    in_specs = [pl.BlockSpec(memory_space=pltpu.SMEM),
                nat_cur(0), nat_cur(1), nat_cur(2),
                dil_cur(0, 1, 4), dil_cur(1, 1, 4), dil_cur(2, 1, 4),
                dil_cur(0, 2, 16), dil_cur(1, 2, 16), dil_cur(2, 2, 16),
                weight_slice(w_ff1), weight_slice(w_ff2)]
    prev_shapes = [(BLOCK, HEAD_DIM), (4, BLOCK, HEAD_DIM), (TILES_PER_SUPER, 16, TM_IN // 16, HEAD_DIM)]
    return pl.pallas_call(
        _mixer_a_kernel,
        grid=grid,
        in_specs=in_specs,
        out_specs=[pl.BlockSpec((None, SUPER, HEAD_DIM), lambda b, h, s: (h, b * nsb + s, 0)),
                   weight_slice(w_ff1), weight_slice(w_ff2)],
        out_shape=[jax.ShapeDtypeStruct((A_HPG, m, HEAD_DIM), BF16),
                   jax.ShapeDtypeStruct(w_ff1.shape, BF16), jax.ShapeDtypeStruct(w_ff2.shape, BF16)],
        scratch_shapes=([pltpu.VMEM((SUPER, HEAD_DIM), F32) for _ in range(6)]
                        + [pltpu.VMEM(shape, BF16) for shape in prev_shapes for _ in range(2)]),
        compiler_params=pltpu.CompilerParams(
            dimension_semantics=("arbitrary", "arbitrary", "arbitrary"), vmem_limit_bytes=VMEM_LIMIT),
        name="mixer_a",
    )(slopes_a, nat, nat, nat, by4, by4, by4, by16, by16, by16, w_ff1, w_ff2)


def _mixer_b_kernel(slopes_ref, sinks_ref, q_ref, kc_ref, kp_ref, vc_ref, vp_ref, *rest, tiles_per_seq):
    n_w = (len(rest) - 1) // 2
    w_refs, o_ref, w_out_refs = rest[:n_w], rest[n_w], rest[n_w + 1:]
    for w_ref, w_out_ref in zip(w_refs, w_out_refs):
        w_out_ref[...] = w_ref[...].astype(BF16)

    first = (pl.program_id(0) % tiles_per_seq) == 0
    kvh = pl.program_id(1)
    tq = kc_ref.shape[0]

    assert B_MAX_BACK == BLOCK - 1
    row = lax.broadcasted_iota(jnp.int32, (BLOCK, BLOCK), 0)
    lane = lax.broadcasted_iota(jnp.int32, (BLOCK, BLOCK), 1)
    from_prev = lane > row
    rel = jnp.where(from_prev, BLOCK + row - lane, row - lane).astype(F32)
    biases, biases_head = [], []
    for hh in range(B_GROUP):
        bias = -slopes_ref[kvh * B_GROUP + hh] * rel
        biases.append(bias)
        biases_head.append(jnp.where(first, jnp.where(from_prev, NEG_INF, bias), bias))

    for blk in range(tq // BLOCK):
        if blk == 0:
            kw = jnp.concatenate([kp_ref[...], kc_ref[0:BLOCK]], axis=0)
            vw = jnp.concatenate([vp_ref[...], vc_ref[0:BLOCK]], axis=0)
        else:
            kw = kc_ref[(blk - 1) * BLOCK:(blk + 1) * BLOCK]
            vw = vc_ref[(blk - 1) * BLOCK:(blk + 1) * BLOCK]
        q = q_ref[:, blk * BLOCK:(blk + 1) * BLOCK, :].reshape(B_GROUP * BLOCK, HEAD_DIM)
        s_all = _dot_nt(q, kw)
        ps, sink_terms = [], []
        for hh in range(B_GROUP):
            s2 = s_all[hh * BLOCK:(hh + 1) * BLOCK]
            s = jnp.where(from_prev, s2[:, :BLOCK], s2[:, BLOCK:]) + (biases_head[hh] if blk == 0 else biases[hh])
            sink = sinks_ref[kvh * B_GROUP + hh]
            mrow = jnp.maximum(jnp.max(s, axis=-1, keepdims=True), sink)
            p = jnp.exp2(s - mrow)
            ps.append(jnp.concatenate([jnp.where(from_prev, p, 0.0).astype(BF16),
                                       jnp.where(from_prev, 0.0, p).astype(BF16)], axis=-1))
            sink_terms.append(jnp.exp2(sink - mrow))
        o_all = _dot(jnp.concatenate(ps, axis=0), _with_ones(vw))
        for hh in range(B_GROUP):
            rows = slice(hh * BLOCK, (hh + 1) * BLOCK)
            o = o_all[rows, :HEAD_DIM] / (o_all[rows, HEAD_DIM:] + sink_terms[hh])
            o_ref[hh, blk * BLOCK:(blk + 1) * BLOCK, :] = o.astype(BF16)


def _mixer_b(heads, slopes_b, sinks, weights, bsz, seq):
    m = bsz * seq
    tq = TQ_B
    per = tq // BLOCK
    grid = (m // tq, B_KV_HEADS)
    n_steps = grid[0] * grid[1]
    assert all(w.shape[0] % (16 * n_steps) == 0 for w in weights)
    nat = heads.reshape(N_HEADS_OUT, m, HEAD_DIM)

    def weight_slice(w):
        return pl.BlockSpec((w.shape[0] // n_steps, w.shape[1]), lambda i, c: (i * B_KV_HEADS + c, 0))

    def cur(head0):
        return pl.BlockSpec((None, tq, HEAD_DIM), lambda i, c: (head0 + c, i, 0))

    def prev(head0):
        return pl.BlockSpec((None, BLOCK, HEAD_DIM), lambda i, c: (head0 + c, jnp.maximum(i * per - 1, 0), 0))

    smem = pl.BlockSpec(memory_space=pltpu.SMEM)
    return pl.pallas_call(
        functools.partial(_mixer_b_kernel, tiles_per_seq=seq // tq),
        grid=grid,
        in_specs=[smem, smem,
                  pl.BlockSpec((B_GROUP, tq, HEAD_DIM), lambda i, c: (QB_HEAD0 // B_GROUP + c, i, 0)),
                  cur(KB_HEAD0), prev(KB_HEAD0), cur(VB_HEAD0), prev(VB_HEAD0)]
                 + [weight_slice(w) for w in weights],
        out_specs=[pl.BlockSpec((B_GROUP, tq, HEAD_DIM), lambda i, c: (c, i, 0))]
                  + [weight_slice(w) for w in weights],
        out_shape=[jax.ShapeDtypeStruct((B_Q_HEADS, m, HEAD_DIM), BF16)]
                  + [jax.ShapeDtypeStruct(w.shape, BF16) for w in weights],
        compiler_params=pltpu.CompilerParams(
            dimension_semantics=("arbitrary", "arbitrary"), vmem_limit_bytes=VMEM_LIMIT),
        name="mixer_b",
    )(slopes_b, sinks, nat, nat, nat, nat, nat, *weights)


def _merge_kernel(oa_ref, ob_ref, gate_ref, wa_ref, wb_ref, out_ref):
    oa = jnp.concatenate([oa_ref[hh] for hh in range(A_HPG)], axis=-1)
    ob = jnp.concatenate([ob_ref[hh] for hh in range(B_Q_HEADS)], axis=-1)
    per = D_MODEL // GATE_W
    ga = jnp.concatenate([gate_ref[GATE_UNIT0 + c] for c in range(per)], axis=-1).astype(F32)
    gb = jnp.concatenate([gate_ref[GATE_UNIT0 + per + c] for c in range(per)], axis=-1).astype(F32)
    merged = ga * _dot(oa, wa_ref[...]) + gb * _dot(ob, wb_ref[...])
    out_ref[...] = merged.astype(BF16)


def _merge(oa, ob, gates, wa, wb):
    m = gates.shape[1]
    tm = TM_MERGE
    resident = dict(pipeline_mode=pl.Buffered(1))
    return pl.pallas_call(
        _merge_kernel,
        grid=(m // tm,),
        in_specs=[pl.BlockSpec((A_HPG, tm, HEAD_DIM), lambda i: (0, i, 0)),
                  pl.BlockSpec((B_Q_HEADS, tm, HEAD_DIM), lambda i: (0, i, 0)),
                  pl.BlockSpec((GATE_UNITS, tm, GATE_W), lambda i: (0, i, 0)),
                  pl.BlockSpec(wa.shape, lambda i: (0, 0), **resident),
                  pl.BlockSpec(wb.shape, lambda i: (0, 0), **resident)],
        out_specs=pl.BlockSpec((tm, D_MODEL), lambda i: (i, 0)),
        out_shape=jax.ShapeDtypeStruct((m, D_MODEL), BF16),
        compiler_params=pltpu.CompilerParams(
            dimension_semantics=("arbitrary",), vmem_limit_bytes=VMEM_LIMIT),
        name="gated_merge",
    )(oa, ob, gates, wa, wb)


def _ffn_kernel(x_ref, mg_ref, wo_ref, g2_ref, w1_ref, w2_ref, out_ref, h2_scr):
    @pl.when(pl.program_id(1) == 0)
    def _():
        x1 = x_ref[...] + _dot(mg_ref[...], wo_ref[...])
        out_ref[...] = x1
        h2_scr[...] = _rms_scale(x1, g2_ref[...]).astype(BF16)

    z = _dot(h2_scr[...], w1_ref[...])
    act = jnp.square(jnp.maximum(z, 0.0)).astype(BF16)
    out_ref[...] += _dot(act, w2_ref[...])


def _ffn(x2, merged, wo, g2, w1, w2):
    m = x2.shape[0]
    tm, tf = TM_FFN, TF_FFN
    return pl.pallas_call(
        _ffn_kernel,
        grid=(m // tm, D_FF // tf),
        in_specs=[pl.BlockSpec((tm, D_MODEL), lambda i, f: (i, 0)),
                  pl.BlockSpec((tm, D_MODEL), lambda i, f: (i, 0)),
                  pl.BlockSpec(wo.shape, lambda i, f: (0, 0), pipeline_mode=pl.Buffered(1)),
                  pl.BlockSpec((1, D_MODEL), lambda i, f: (0, 0)),
                  pl.BlockSpec((D_MODEL, tf), lambda i, f: (0, f)),
                  pl.BlockSpec((tf, D_MODEL), lambda i, f: (f, 0))],
        out_specs=pl.BlockSpec((tm, D_MODEL), lambda i, f: (i, 0)),
        out_shape=jax.ShapeDtypeStruct((m, D_MODEL), F32),
        scratch_shapes=[pltpu.VMEM((tm, D_MODEL), BF16)],
        compiler_params=pltpu.CompilerParams(
            dimension_semantics=("arbitrary", "arbitrary"), vmem_limit_bytes=VMEM_LIMIT),
        name="out_proj_ffn",
    )(x2, merged, wo, g2, w1, w2)


def kernel(x, norm1_g, w_in, q_norm_a, k_norm_a, q_norm_b, k_norm_b, sinks_b, w_branch_a, w_branch_b,
           w_out, norm2_g, w_ff1, w_ff2):
    bsz, seq, d_model = x.shape
    depth = w_in.shape[0]
    assert d_model == D_MODEL and w_in.shape[2] == IN_COLS
    assert seq % SUPER == 0 and seq % TQ_B == 0 and SUPER % TM_IN == 0
    m = bsz * seq
    slopes = _alibi_slopes() * np.float32(LOG2E)
    slopes_b = jnp.asarray(slopes[:B_Q_HEADS])
    slopes_a = jnp.asarray(slopes[B_Q_HEADS:])
    scale = HEAD_DIM ** -0.5 * LOG2E

    x2 = x.reshape(m, D_MODEL)
    for l in range(depth):
        qk_gains = jnp.stack([q_norm_a[l] * scale, k_norm_a[l], q_norm_b[l] * scale, k_norm_b[l]])
        w_tail = w_in[l][:, TAIL_TILE * TN_IN:].astype(BF16)
        heads, gates = _in_projection(x2, norm1_g[l].reshape(1, D_MODEL), w_in[l].astype(BF16), w_tail, qk_gains)
        oa, w1_bf16, w2_bf16 = _mixer_a(heads, slopes_a, w_ff1[l], w_ff2[l], bsz, seq)
        ob, wa_bf16, wb_bf16, wo_bf16 = _mixer_b(heads, slopes_b, sinks_b[l] * LOG2E,
                                                 (w_branch_a[l], w_branch_b[l], w_out[l]), bsz, seq)
        merged = _merge(oa, ob, gates, wa_bf16, wb_bf16)
        x2 = _ffn(x2, merged, wo_bf16, norm2_g[l].reshape(1, D_MODEL),
                  w1_bf16, w2_bf16)
    return x2.reshape(bsz, seq, D_MODEL)
```

```python
import functools

import numpy as np
import jax
import jax.numpy as jnp
from jax import lax
from jax.experimental import pallas as pl
from jax.experimental.pallas import tpu as pltpu

D_MODEL = 2048
HEAD_DIM = 128
BLOCK = 128
EPS = 1e-6
DILATIONS = (1, 4, 16)
A_MAX_BACK = 128
N_GROUPS = 3
A_HPG = 8
A_WIDTH = N_GROUPS * A_HPG * HEAD_DIM
B_Q_HEADS = 16
B_KV_HEADS = 2
B_GROUP = B_Q_HEADS // B_KV_HEADS
B_MAX_BACK = 127
D_FF = 4 * D_MODEL
N_ATTN_HEADS = B_Q_HEADS + N_GROUPS * A_HPG
IN_COLS = 3 * A_WIDTH + B_Q_HEADS * HEAD_DIM + 2 * B_KV_HEADS * HEAD_DIM + 2 * D_MODEL

SUPER = BLOCK * DILATIONS[-1]
NEG_INF = float("-inf")
LOG2E = 1.4426950408889634
BF16 = jnp.bfloat16
F32 = jnp.float32

VMEM_LIMIT = 56 * 1024 * 1024

TM_IN = 1024
TN_IN = 1024
HEADS_PER_TILE = TN_IN // HEAD_DIM
A_TILES = 3 * N_GROUPS
QB_TILE0 = A_TILES
QB_TILES = B_Q_HEADS // HEADS_PER_TILE
KVB_TILE = QB_TILE0 + QB_TILES
N_TILES_IN = -(-IN_COLS // TN_IN)
TAIL_TILE = N_TILES_IN - 1
TAIL_COLS = IN_COLS - TAIL_TILE * TN_IN
N_HEAD_TILES = KVB_TILE + 1
N_HEADS_OUT = N_HEAD_TILES * HEADS_PER_TILE
QB_HEAD0 = QB_TILE0 * HEADS_PER_TILE
KB_HEAD0 = KVB_TILE * HEADS_PER_TILE
VB_HEAD0 = KB_HEAD0 + B_KV_HEADS
GATE_W = TN_IN // 2
GATE_UNITS = 2 * (N_TILES_IN - KVB_TILE)
GATE_UNIT0 = 1

TQ_B = 2048
TM_MERGE = 512
TM_FFN = 512
TF_FFN = 1024


def _alibi_slopes():
    i = np.arange(1, N_ATTN_HEADS + 1, dtype=np.float32)
    return (2.0 ** (-8.0 * i / N_ATTN_HEADS)).astype(np.float32)


def _rms_scale(a, gain):
    ms = jnp.mean(a * a, axis=-1, keepdims=True)
    return a * lax.rsqrt(ms + EPS) * gain


def _dot(a, b):
    return jnp.dot(a, b, preferred_element_type=F32)


def _dot_nt(a, b):
    return lax.dot_general(a, b, (((1,), (1,)), ((), ())), preferred_element_type=F32)


def _inproj_kernel(x_ref, g1_ref, w_ref, wtail_ref, qkg_ref, heads_ref, gate_ref, h_scr, acc_scr, tmp_scr, *, n_steps):
    s = pl.program_id(0)
    p = s % N_TILES_IN
    t = jnp.maximum(s - 1, 0) % N_TILES_IN - 1
    has_matmul = s < n_steps - 1
    tm = x_ref.shape[0]

    def matmul_tile():
        acc = _dot(h_scr[...], w_ref[...])
        for c in range(HEADS_PER_TILE):
            acc_scr[c] = acc[:, c * HEAD_DIM:(c + 1) * HEAD_DIM]

    def matmul_tail():
        acc = _dot(h_scr[...], wtail_ref[...])
        for c in range(TAIL_COLS // HEAD_DIM):
            acc_scr[c] = acc[:, c * HEAD_DIM:(c + 1) * HEAD_DIM]

    def write_head(c, d, gain):
        def emit(a, res, rows):
            if gain is not None:
                a = _rms_scale(a, gain)
            heads_ref[c, res * rows:(res + 1) * rows, :] = a.astype(BF16)

        if d == 1:
            emit(acc_scr[c], 0, tm)
        elif d == 4:
            for res in range(4):
                emit(acc_scr[c, pl.ds(res, tm // 4, stride=4), :], res, tm // 4)
        else:
            tmp = tmp_scr.at[c % 2]
            for r4 in range(4):
                tmp[r4 * (tm // 4):(r4 + 1) * (tm // 4), :] = acc_scr[c, pl.ds(r4, tm // 4, stride=4), :]
            for r4 in range(4):
                for q in range(4):
                    emit(tmp[pl.ds(r4 * (tm // 4) + q, tm // 16, stride=4), :], r4 + 4 * q, tm // 16)

    def write_gates(c0, unit):
        for c in range(GATE_W // HEAD_DIM):
            z = acc_scr[c0 + c]
            gate_ref[unit, :, c * HEAD_DIM:(c + 1) * HEAD_DIM] = (0.5 + 0.5 * jnp.tanh(0.5 * z)).astype(BF16)

    def write_gate_tile():
        write_gates(0, 0)
        write_gates(GATE_W // HEAD_DIM, 1)

    def write_gate_tail():
        write_gates(0, 0)
        gate_ref[1] = jnp.zeros(gate_ref.shape[1:], BF16)

    @pl.when(has_matmul & (p == 0))
    def _():
        @pl.when(s > 0)
        def _():
            write_gate_tile()
        rows = 256
        for c in range(tm // rows):
            x = x_ref[c * rows:(c + 1) * rows, :]
            h_scr[c * rows:(c + 1) * rows, :] = _rms_scale(x, g1_ref[...]).astype(BF16)
        matmul_tail()

    @pl.when(s == n_steps - 1)
    def _():
        write_gate_tile()

    @pl.when(has_matmul & (p == 1))
    def _():
        write_gate_tail()
        matmul_tile()

    mid_row = has_matmul & (p > 1)
    role = t // N_GROUPS
    group = t % N_GROUPS
    is_a = mid_row & (t < A_TILES)

    for g, d in enumerate(DILATIONS):
        @pl.when(is_a & (group == g) & (role < 2))
        def _(d=d):
            gain = jnp.where(role == 0, qkg_ref[0:1, :], qkg_ref[1:2, :])
            for c in range(HEADS_PER_TILE):
                write_head(c, d, gain)
            matmul_tile()

        @pl.when(is_a & (group == g) & (role == 2))
        def _(d=d):
            for c in range(HEADS_PER_TILE):
                write_head(c, d, None)
            matmul_tile()

    @pl.when(mid_row & (t >= QB_TILE0) & (t < KVB_TILE))
    def _():
        for c in range(HEADS_PER_TILE):
            write_head(c, 1, qkg_ref[2:3, :])
        matmul_tile()

    @pl.when(mid_row & (t == KVB_TILE))
    def _():
        for c in range(B_KV_HEADS):
            write_head(c, 1, qkg_ref[3:4, :])
        for c in range(B_KV_HEADS, HEADS_PER_TILE):
            write_head(c, 1, None)
        write_gate_tile()
        matmul_tile()

    @pl.when(mid_row & (t > KVB_TILE))
    def _():
        write_gate_tile()
        matmul_tile()


def _in_projection(x2, g1, w, w_tail, qk_gains):
    m = x2.shape[0]
    tm = TM_IN
    n_rows = m // tm
    n_steps = n_rows * N_TILES_IN + 1

    def row_of(s):
        return jnp.maximum(s - 1, 0) // N_TILES_IN

    def tile_of(s):
        return jnp.maximum(s - 1, 0) % N_TILES_IN - 1

    out_shapes = [
        jax.ShapeDtypeStruct((N_HEADS_OUT, n_rows, tm, HEAD_DIM), BF16),
        jax.ShapeDtypeStruct((GATE_UNITS, m, GATE_W), BF16),
    ]
    out_specs = [
        pl.BlockSpec((HEADS_PER_TILE, None, tm, HEAD_DIM),
                     lambda s: (jnp.clip(tile_of(s), 0, KVB_TILE), row_of(s), 0, 0)),
        pl.BlockSpec((2, tm, GATE_W),
                     lambda s: (jnp.where(tile_of(s) >= KVB_TILE, tile_of(s) - KVB_TILE, TAIL_TILE - KVB_TILE),
                                row_of(s), 0)),
    ]
    in_specs = [
        pl.BlockSpec((tm, D_MODEL), lambda s: (jnp.minimum(s // N_TILES_IN, n_rows - 1), 0)),
        pl.BlockSpec((1, D_MODEL), lambda s: (0, 0)),
        pl.BlockSpec((D_MODEL, TN_IN),
                     lambda s: (0, jnp.where(s < n_steps - 1, jnp.maximum(s % N_TILES_IN - 1, 0),
                                             TAIL_TILE - 1))),
        pl.BlockSpec((D_MODEL, TAIL_COLS), lambda s: (0, 0), pipeline_mode=pl.Buffered(1)),
        pl.BlockSpec((4, HEAD_DIM), lambda s: (0, 0)),
    ]
    return pl.pallas_call(
        functools.partial(_inproj_kernel, n_steps=n_steps),
        grid=(n_steps,),
        in_specs=in_specs,
        out_specs=out_specs,
        out_shape=out_shapes,
        scratch_shapes=[pltpu.VMEM((tm, D_MODEL), BF16),
                        pltpu.VMEM((HEADS_PER_TILE, tm, HEAD_DIM), F32),
                        pltpu.VMEM((2, tm, HEAD_DIM), F32)],
        compiler_params=pltpu.CompilerParams(
            dimension_semantics=("arbitrary",), vmem_limit_bytes=VMEM_LIMIT),
        name="in_projection",
    )(x2, g1, w, w_tail, qk_gains)


def _band_bias(slope_times_stride, max_back):
    qi = lax.broadcasted_iota(jnp.int32, (BLOCK, 2 * BLOCK), 0) + BLOCK
    ki = lax.broadcasted_iota(jnp.int32, (BLOCK, 2 * BLOCK), 1)
    rel = qi - ki
    valid = (rel >= 0) & (rel <= max_back)
    bias = jnp.where(valid, -slope_times_stride * rel.astype(F32), NEG_INF)
    return bias, ki


def _with_ones(vw):
    return jnp.concatenate([vw, jnp.ones(vw.shape, vw.dtype)], axis=-1)


def _attn_units(units):
    scores = [_dot_nt(q, kw) + bias for q, kw, _, bias in units]
    maxes = [jnp.max(s, axis=-1, keepdims=True) for s in scores]
    probs = [jnp.exp2(s - m).astype(BF16) for s, m in zip(scores, maxes)]
    outs = [_dot(p, _with_ones(u[2])) for p, u in zip(probs, units)]
    results = []
    for o_l, m in zip(outs, maxes):
        o, l = o_l[:, :HEAD_DIM], o_l[:, HEAD_DIM:]
        results.append((o / l, m + jnp.log2(l)))
    return results


UNITS_PER_BATCH = 16
TILES_PER_SUPER = SUPER // TM_IN


A_HEADS_PER_STEP = 2


def _mixer_a_kernel(slopes_ref, *refs):
    qkv_refs, (w1_ref, w2_ref) = refs[:9], refs[9:11]
    o_ref, w1_out_ref, w2_out_ref = refs[11:14]
    work_scrs, prev_refs = refs[14:20], refs[20:26]
    w1_out_ref[...] = w1_ref[...].astype(BF16)
    w2_out_ref[...] = w2_ref[...].astype(BF16)

    first = pl.program_id(2) == 0

    @pl.when(first)
    def _():
        for ref in prev_refs:
            ref[...] = jnp.zeros(ref.shape, ref.dtype)

    for hh in range(A_HEADS_PER_STEP):
        _mixer_a_head(slopes_ref, pl.program_id(1) * A_HEADS_PER_STEP + hh, first,
                      *[ref.at[hh] for ref in qkv_refs], o_ref.at[hh], *work_scrs,
                      *[ref.at[hh] for ref in prev_refs])


def _mixer_a_head(slopes_ref, h, first, q0_ref, k0c_ref, v0c_ref, q1_ref, k1c_ref, v1c_ref, q2_ref, k2c_ref, v2c_ref,
                  o_ref, o1_scr, l1_scr, o2_scr, l2_scr, o4_scr, l4_scr,
                  k0p_ref, v0p_ref, k1p_ref, v1p_ref, k2p_ref, v2p_ref):
    biases = []
    for g, d in enumerate(DILATIONS):
        bias, ki = _band_bias(slopes_ref[g * A_HPG + h] * float(d), A_MAX_BACK)
        bias_head = jnp.where(first, jnp.where(ki < BLOCK, NEG_INF, bias), bias)
        biases.append((bias, bias_head))

    def cat(*parts):
        return jnp.concatenate(parts, axis=0)

    def block16(ref, r):
        return cat(ref[0, r], ref[1, r])

    def body2(it, carry):
        rs = [it * UNITS_PER_BATCH + u for u in range(UNITS_PER_BATCH)]
        units = [(block16(q2_ref, r), cat(block16(k2p_ref, r), block16(k2c_ref, r)),
                  cat(block16(v2p_ref, r), block16(v2c_ref, r)), biases[2][1]) for r in rs]
        for r, (o, lse) in zip(rs, _attn_units(units)):
            rows = pl.ds((r % 4) * (SUPER // 4) + r // 4, BLOCK, stride=4)
            o4_scr[rows, :] = o
            l4_scr[rows, :] = lse
        return carry

    lax.fori_loop(0, 16 // UNITS_PER_BATCH, body2, 0)
    for r4 in range(4):
        seg = slice(r4 * (SUPER // 4), (r4 + 1) * (SUPER // 4))
        o2_scr[pl.ds(r4, SUPER // 4, stride=4), :] = o4_scr[seg, :]
        l2_scr[pl.ds(r4, SUPER // 4, stride=4), :] = l4_scr[seg, :]

    def block4(ref, r, blk):
        half = (blk % 2) * BLOCK
        return ref[blk // 2, r, half:half + BLOCK]

    def window4(cur_ref, prev_ref, r, blk):
        prev = prev_ref[r] if blk == 0 else block4(cur_ref, r, blk - 1)
        return cat(prev, block4(cur_ref, r, blk))

    def body1(it, carry):
        per = UNITS_PER_BATCH // 4
        rb = [(it * per + u // 4, u % 4) for u in range(UNITS_PER_BATCH)]
        units = [(block4(q1_ref, r, blk), window4(k1c_ref, k1p_ref, r, blk), window4(v1c_ref, v1p_ref, r, blk),
                  biases[1][1] if blk == 0 else biases[1][0]) for r, blk in rb]
        for (r, blk), (o, lse) in zip(rb, _attn_units(units)):
            rows = pl.ds(blk * BLOCK * 4 + r, BLOCK, stride=4)
            o1_scr[rows, :] = o
            l1_scr[rows, :] = lse
        return carry

    lax.fori_loop(0, 16 // UNITS_PER_BATCH, body1, 0)

    def window1(cur_ref, prev_ref, blk):
        if blk == 0:
            return cat(prev_ref[...], cur_ref[0:BLOCK])
        return cur_ref[(blk - 1) * BLOCK:(blk + 1) * BLOCK]

    for b0 in range(0, SUPER // BLOCK, 8):
        blks = list(range(b0, b0 + 8))
        units = [(q0_ref[blk * BLOCK:(blk + 1) * BLOCK, :], window1(k0c_ref, k0p_ref, blk),
                  window1(v0c_ref, v0p_ref, blk), biases[0][1] if blk == 0 else biases[0][0])
                 for blk in blks]
        for blk, (o0, lse0) in zip(blks, _attn_units(units)):
            rows = slice(blk * BLOCK, (blk + 1) * BLOCK)
            lse1 = l1_scr[rows, :]
            lse2 = l2_scr[rows, :]
            mx = jnp.maximum(jnp.maximum(lse0, lse1), lse2)
            w0 = jnp.exp2(lse0 - mx)
            w1 = jnp.exp2(lse1 - mx)
            w2 = jnp.exp2(lse2 - mx)
            o = (w0 * o0 + w1 * o1_scr[rows, :] + w2 * o2_scr[rows, :]) / (w0 + w1 + w2)
            o_ref[rows, :] = o.astype(BF16)

    k0p_ref[...] = k0c_ref[SUPER - BLOCK:SUPER]
    v0p_ref[...] = v0c_ref[SUPER - BLOCK:SUPER]
    k1p_ref[...] = k1c_ref[TILES_PER_SUPER - 1, :, TM_IN // 4 - BLOCK:TM_IN // 4]
    v1p_ref[...] = v1c_ref[TILES_PER_SUPER - 1, :, TM_IN // 4 - BLOCK:TM_IN // 4]
    k2p_ref[...] = k2c_ref[...]
    v2p_ref[...] = v2c_ref[...]


def _mixer_a(heads, slopes_a, w_ff1, w_ff2, bsz, seq):
    m = bsz * seq
    nsb = seq // SUPER
    n_rows = m // TM_IN
    hs = A_HEADS_PER_STEP
    pairs = A_HPG // hs
    grid = (bsz, pairs, nsb)
    n_steps = bsz * nsb * pairs
    assert w_ff1.shape[0] % (16 * n_steps) == 0 and w_ff2.shape[0] % (16 * n_steps) == 0

    def weight_slice(w):
        return pl.BlockSpec((w.shape[0] // n_steps, w.shape[1]), lambda b, h, s: ((b * pairs + h) * nsb + s, 0))
    nat = heads.reshape(N_HEADS_OUT, m, HEAD_DIM)
    by4 = heads.reshape(N_HEADS_OUT, n_rows, 4, TM_IN // 4, HEAD_DIM)
    by16 = heads.reshape(N_HEADS_OUT, n_rows, 16, TM_IN // 16, HEAD_DIM)

    def head(role, g, h):
        return (role * N_GROUPS + g) * pairs + h

    def nat_cur(role):
        return pl.BlockSpec((hs, SUPER, HEAD_DIM), lambda b, h, s: (head(role, 0, h), b * nsb + s, 0))

    def dil_cur(role, g, d):
        return pl.BlockSpec((hs, TILES_PER_SUPER, d, TM_IN // d, HEAD_DIM),
                            lambda b, h, s: (head(role, g, h), b * nsb + s, 0, 0, 0))

    in_specs = [pl.BlockSpec(memory_space=pltpu.SMEM),
                nat_cur(0), nat_cur(1), nat_cur(2),
                dil_cur(0, 1, 4), dil_cur(1, 1, 4), dil_cur(2, 1, 4),
                dil_cur(0, 2, 16), dil_cur(1, 2, 16), dil_cur(2, 2, 16),
                weight_slice(w_ff1), weight_slice(w_ff2)]
    prev_shapes = [(hs, BLOCK, HEAD_DIM), (hs, 4, BLOCK, HEAD_DIM), (hs, TILES_PER_SUPER, 16, TM_IN // 16, HEAD_DIM)]
    return pl.pallas_call(
        _mixer_a_kernel,
        grid=grid,
        in_specs=in_specs,
        out_specs=[pl.BlockSpec((hs, SUPER, HEAD_DIM), lambda b, h, s: (h, b * nsb + s, 0)),
                   weight_slice(w_ff1), weight_slice(w_ff2)],
        out_shape=[jax.ShapeDtypeStruct((A_HPG, m, HEAD_DIM), BF16),
                   jax.ShapeDtypeStruct(w_ff1.shape, BF16), jax.ShapeDtypeStruct(w_ff2.shape, BF16)],
        scratch_shapes=([pltpu.VMEM((SUPER, HEAD_DIM), F32) for _ in range(6)]
                        + [pltpu.VMEM(shape, BF16) for shape in prev_shapes for _ in range(2)]),
        compiler_params=pltpu.CompilerParams(
            dimension_semantics=("arbitrary", "arbitrary", "arbitrary"), vmem_limit_bytes=VMEM_LIMIT),
        name="mixer_a",
    )(slopes_a, nat, nat, nat, by4, by4, by4, by16, by16, by16, w_ff1, w_ff2)


def _mixer_b_kernel(slopes_ref, sinks_ref, q_ref, kc_ref, kp_ref, vc_ref, vp_ref, *rest, tiles_per_seq):
    n_w = (len(rest) - 1) // 2
    w_refs, o_ref, w_out_refs = rest[:n_w], rest[n_w], rest[n_w + 1:]
    for w_ref, w_out_ref in zip(w_refs, w_out_refs):
        w_out_ref[...] = w_ref[...].astype(BF16)

    first = (pl.program_id(0) % tiles_per_seq) == 0
    kvh = pl.program_id(1)
    tq = kc_ref.shape[0]

    assert B_MAX_BACK == BLOCK - 1
    row = lax.broadcasted_iota(jnp.int32, (BLOCK, BLOCK), 0)
    lane = lax.broadcasted_iota(jnp.int32, (BLOCK, BLOCK), 1)
    from_prev = lane > row
    rel = jnp.where(from_prev, BLOCK + row - lane, row - lane).astype(F32)
    biases, biases_head = [], []
    for hh in range(B_GROUP):
        bias = -slopes_ref[kvh * B_GROUP + hh] * rel
        biases.append(bias)
        biases_head.append(jnp.where(first, jnp.where(from_prev, NEG_INF, bias), bias))

    for blk in range(tq // BLOCK):
        if blk == 0:
            kw = jnp.concatenate([kp_ref[...], kc_ref[0:BLOCK]], axis=0)
            vw = jnp.concatenate([vp_ref[...], vc_ref[0:BLOCK]], axis=0)
        else:
            kw = kc_ref[(blk - 1) * BLOCK:(blk + 1) * BLOCK]
            vw = vc_ref[(blk - 1) * BLOCK:(blk + 1) * BLOCK]
        q = q_ref[:, blk * BLOCK:(blk + 1) * BLOCK, :].reshape(B_GROUP * BLOCK, HEAD_DIM)
        s_all = _dot_nt(q, kw)
        ps, sink_terms = [], []
        for hh in range(B_GROUP):
            s2 = s_all[hh * BLOCK:(hh + 1) * BLOCK]
            s = jnp.where(from_prev, s2[:, :BLOCK], s2[:, BLOCK:]) + (biases_head[hh] if blk == 0 else biases[hh])
            sink = sinks_ref[kvh * B_GROUP + hh]
            mrow = jnp.maximum(jnp.max(s, axis=-1, keepdims=True), sink)
            p = jnp.exp2(s - mrow)
            ps.append(jnp.concatenate([jnp.where(from_prev, p, 0.0).astype(BF16),
                                       jnp.where(from_prev, 0.0, p).astype(BF16)], axis=-1))
            sink_terms.append(jnp.exp2(sink - mrow))
        o_all = _dot(jnp.concatenate(ps, axis=0), _with_ones(vw))
        for hh in range(B_GROUP):
            rows = slice(hh * BLOCK, (hh + 1) * BLOCK)
            o = o_all[rows, :HEAD_DIM] / (o_all[rows, HEAD_DIM:] + sink_terms[hh])
            o_ref[hh, blk * BLOCK:(blk + 1) * BLOCK, :] = o.astype(BF16)


def _mixer_b(heads, slopes_b, sinks, weights, bsz, seq):
    m = bsz * seq
    tq = TQ_B
    per = tq // BLOCK
    grid = (m // tq, B_KV_HEADS)
    n_steps = grid[0] * grid[1]
    assert all(w.shape[0] % (16 * n_steps) == 0 for w in weights)
    nat = heads.reshape(N_HEADS_OUT, m, HEAD_DIM)

    def weight_slice(w):
        return pl.BlockSpec((w.shape[0] // n_steps, w.shape[1]), lambda i, c: (i * B_KV_HEADS + c, 0))

    def cur(head0):
        return pl.BlockSpec((None, tq, HEAD_DIM), lambda i, c: (head0 + c, i, 0))

    def prev(head0):
        return pl.BlockSpec((None, BLOCK, HEAD_DIM), lambda i, c: (head0 + c, jnp.maximum(i * per - 1, 0), 0))

    smem = pl.BlockSpec(memory_space=pltpu.SMEM)
    return pl.pallas_call(
        functools.partial(_mixer_b_kernel, tiles_per_seq=seq // tq),
        grid=grid,
        in_specs=[smem, smem,
                  pl.BlockSpec((B_GROUP, tq, HEAD_DIM), lambda i, c: (QB_HEAD0 // B_GROUP + c, i, 0)),
                  cur(KB_HEAD0), prev(KB_HEAD0), cur(VB_HEAD0), prev(VB_HEAD0)]
                 + [weight_slice(w) for w in weights],
        out_specs=[pl.BlockSpec((B_GROUP, tq, HEAD_DIM), lambda i, c: (c, i, 0))]
                  + [weight_slice(w) for w in weights],
        out_shape=[jax.ShapeDtypeStruct((B_Q_HEADS, m, HEAD_DIM), BF16)]
                  + [jax.ShapeDtypeStruct(w.shape, BF16) for w in weights],
        compiler_params=pltpu.CompilerParams(
            dimension_semantics=("arbitrary", "arbitrary"), vmem_limit_bytes=VMEM_LIMIT),
        name="mixer_b",
    )(slopes_b, sinks, nat, nat, nat, nat, nat, *weights)


def _merge_kernel(oa_ref, ob_ref, gate_ref, wa_ref, wb_ref, out_ref):
    oa = jnp.concatenate([oa_ref[hh] for hh in range(A_HPG)], axis=-1)
    ob = jnp.concatenate([ob_ref[hh] for hh in range(B_Q_HEADS)], axis=-1)
    per = D_MODEL // GATE_W
    ga = jnp.concatenate([gate_ref[GATE_UNIT0 + c] for c in range(per)], axis=-1).astype(F32)
    gb = jnp.concatenate([gate_ref[GATE_UNIT0 + per + c] for c in range(per)], axis=-1).astype(F32)
    merged = ga * _dot(oa, wa_ref[...]) + gb * _dot(ob, wb_ref[...])
    out_ref[...] = merged.astype(BF16)


def _merge(oa, ob, gates, wa, wb):
    m = gates.shape[1]
    tm = TM_MERGE
    resident = dict(pipeline_mode=pl.Buffered(1))
    return pl.pallas_call(
        _merge_kernel,
        grid=(m // tm,),
        in_specs=[pl.BlockSpec((A_HPG, tm, HEAD_DIM), lambda i: (0, i, 0)),
                  pl.BlockSpec((B_Q_HEADS, tm, HEAD_DIM), lambda i: (0, i, 0)),
                  pl.BlockSpec((GATE_UNITS, tm, GATE_W), lambda i: (0, i, 0)),
                  pl.BlockSpec(wa.shape, lambda i: (0, 0), **resident),
                  pl.BlockSpec(wb.shape, lambda i: (0, 0), **resident)],
        out_specs=pl.BlockSpec((tm, D_MODEL), lambda i: (i, 0)),
        out_shape=jax.ShapeDtypeStruct((m, D_MODEL), BF16),
        compiler_params=pltpu.CompilerParams(
            dimension_semantics=("arbitrary",), vmem_limit_bytes=VMEM_LIMIT),
        name="gated_merge",
    )(oa, ob, gates, wa, wb)


def _ffn_kernel(x_ref, mg_ref, wo_ref, g2_ref, w1_ref, w2_ref, out_ref, h2_scr):
    @pl.when(pl.program_id(1) == 0)
    def _():
        x1 = x_ref[...] + _dot(mg_ref[...], wo_ref[...])
        out_ref[...] = x1
        h2_scr[...] = _rms_scale(x1, g2_ref[...]).astype(BF16)

    z = _dot(h2_scr[...], w1_ref[...])
    act = jnp.square(jnp.maximum(z, 0.0)).astype(BF16)
    out_ref[...] += _dot(act, w2_ref[...])


def _ffn(x2, merged, wo, g2, w1, w2):
    m = x2.shape[0]
    tm, tf = TM_FFN, TF_FFN
    return pl.pallas_call(
        _ffn_kernel,
        grid=(m // tm, D_FF // tf),
        in_specs=[pl.BlockSpec((tm, D_MODEL), lambda i, f: (i, 0)),
                  pl.BlockSpec((tm, D_MODEL), lambda i, f: (i, 0)),
                  pl.BlockSpec(wo.shape, lambda i, f: (0, 0), pipeline_mode=pl.Buffered(1)),
                  pl.BlockSpec((1, D_MODEL), lambda i, f: (0, 0)),
                  pl.BlockSpec((D_MODEL, tf), lambda i, f: (0, f)),
                  pl.BlockSpec((tf, D_MODEL), lambda i, f: (f, 0))],
        out_specs=pl.BlockSpec((tm, D_MODEL), lambda i, f: (i, 0)),
        out_shape=jax.ShapeDtypeStruct((m, D_MODEL), F32),
        scratch_shapes=[pltpu.VMEM((tm, D_MODEL), BF16)],
        compiler_params=pltpu.CompilerParams(
            dimension_semantics=("arbitrary", "arbitrary"), vmem_limit_bytes=VMEM_LIMIT),
        name="out_proj_ffn",
    )(x2, merged, wo, g2, w1, w2)


def kernel(x, norm1_g, w_in, q_norm_a, k_norm_a, q_norm_b, k_norm_b, sinks_b, w_branch_a, w_branch_b,
           w_out, norm2_g, w_ff1, w_ff2):
    bsz, seq, d_model = x.shape
    depth = w_in.shape[0]
    assert d_model == D_MODEL and w_in.shape[2] == IN_COLS
    assert seq % SUPER == 0 and seq % TQ_B == 0 and SUPER % TM_IN == 0
    m = bsz * seq
    slopes = _alibi_slopes() * np.float32(LOG2E)
    slopes_b = jnp.asarray(slopes[:B_Q_HEADS])
    slopes_a = jnp.asarray(slopes[B_Q_HEADS:])
    scale = HEAD_DIM ** -0.5 * LOG2E

    x2 = x.reshape(m, D_MODEL)
    for l in range(depth):
        qk_gains = jnp.stack([q_norm_a[l] * scale, k_norm_a[l], q_norm_b[l] * scale, k_norm_b[l]])
        w_tail = w_in[l][:, TAIL_TILE * TN_IN:].astype(BF16)
        heads, gates = _in_projection(x2, norm1_g[l].reshape(1, D_MODEL), w_in[l].astype(BF16), w_tail, qk_gains)
        oa, w1_bf16, w2_bf16 = _mixer_a(heads, slopes_a, w_ff1[l], w_ff2[l], bsz, seq)
        ob, wa_bf16, wb_bf16, wo_bf16 = _mixer_b(heads, slopes_b, sinks_b[l] * LOG2E,
                                                 (w_branch_a[l], w_branch_b[l], w_out[l]), bsz, seq)
        merged = _merge(oa, ob, gates, wa_bf16, wb_bf16)
        x2 = _ffn(x2, merged, wo_bf16, norm2_g[l].reshape(1, D_MODEL),
                  w1_bf16, w2_bf16)
    return x2.reshape(bsz, seq, D_MODEL)
```

```python
import functools

import numpy as np
import jax
import jax.numpy as jnp
from jax import lax
from jax.experimental import pallas as pl
from jax.experimental.pallas import tpu as pltpu

D_MODEL = 2048
HEAD_DIM = 128
BLOCK = 128
EPS = 1e-6
DILATIONS = (1, 4, 16)
A_MAX_BACK = 128
N_GROUPS = 3
A_HPG = 8
A_WIDTH = N_GROUPS * A_HPG * HEAD_DIM
B_Q_HEADS = 16
B_KV_HEADS = 2
B_GROUP = B_Q_HEADS // B_KV_HEADS
B_MAX_BACK = 127
D_FF = 4 * D_MODEL
N_ATTN_HEADS = B_Q_HEADS + N_GROUPS * A_HPG
IN_COLS = 3 * A_WIDTH + B_Q_HEADS * HEAD_DIM + 2 * B_KV_HEADS * HEAD_DIM + 2 * D_MODEL

SUPER = BLOCK * DILATIONS[-1]
NEG_INF = float("-inf")
LOG2E = 1.4426950408889634
BF16 = jnp.bfloat16
F32 = jnp.float32

VMEM_LIMIT = 56 * 1024 * 1024

TM_IN = 1024
TN_IN = 1024
HEADS_PER_TILE = TN_IN // HEAD_DIM
A_TILES = 3 * N_GROUPS
QB_TILE0 = A_TILES
QB_TILES = B_Q_HEADS // HEADS_PER_TILE
KVB_TILE = QB_TILE0 + QB_TILES
N_TILES_IN = -(-IN_COLS // TN_IN)
TAIL_TILE = N_TILES_IN - 1
TAIL_COLS = IN_COLS - TAIL_TILE * TN_IN
N_HEAD_TILES = KVB_TILE + 1
N_HEADS_OUT = N_HEAD_TILES * HEADS_PER_TILE
QB_HEAD0 = QB_TILE0 * HEADS_PER_TILE
KB_HEAD0 = KVB_TILE * HEADS_PER_TILE
VB_HEAD0 = KB_HEAD0 + B_KV_HEADS
GATE_W = TN_IN // 2
GATE_UNITS = 2 * (N_TILES_IN - KVB_TILE)
GATE_UNIT0 = 1

TQ_B = 2048
TM_MERGE = 512
TM_FFN = 1024
TF_FFN = 512


def _alibi_slopes():
    i = np.arange(1, N_ATTN_HEADS + 1, dtype=np.float32)
    return (2.0 ** (-8.0 * i / N_ATTN_HEADS)).astype(np.float32)


def _rms_scale(a, gain):
    ms = jnp.mean(a * a, axis=-1, keepdims=True)
    return a * lax.rsqrt(ms + EPS) * gain


def _dot(a, b):
    return jnp.dot(a, b, preferred_element_type=F32)


def _dot_nt(a, b):
    return lax.dot_general(a, b, (((1,), (1,)), ((), ())), preferred_element_type=F32)


def _inproj_kernel(x_ref, g1_ref, w_ref, wtail_ref, qkg_ref, heads_ref, gate_ref, h_scr, acc_scr, tmp_scr, *, n_steps):
    s = pl.program_id(0)
    p = s % N_TILES_IN
    t = jnp.maximum(s - 1, 0) % N_TILES_IN - 1
    has_matmul = s < n_steps - 1
    tm = x_ref.shape[0]

    def matmul_tile():
        acc = _dot(h_scr[...], w_ref[...])
        for c in range(HEADS_PER_TILE):
            acc_scr[c] = acc[:, c * HEAD_DIM:(c + 1) * HEAD_DIM]

    def matmul_tail():
        acc = _dot(h_scr[...], wtail_ref[...])
        for c in range(TAIL_COLS // HEAD_DIM):
            acc_scr[c] = acc[:, c * HEAD_DIM:(c + 1) * HEAD_DIM]

    def write_head(c, d, gain):
        def emit(a, res, rows):
            if gain is not None:
                a = _rms_scale(a, gain)
            heads_ref[c, res * rows:(res + 1) * rows, :] = a.astype(BF16)

        if d == 1:
            emit(acc_scr[c], 0, tm)
        elif d == 4:
            for res in range(4):
                emit(acc_scr[c, pl.ds(res, tm // 4, stride=4), :], res, tm // 4)
        else:
            tmp = tmp_scr.at[c % 2]
            for r4 in range(4):
                tmp[r4 * (tm // 4):(r4 + 1) * (tm // 4), :] = acc_scr[c, pl.ds(r4, tm // 4, stride=4), :]
            for r4 in range(4):
                for q in range(4):
                    emit(tmp[pl.ds(r4 * (tm // 4) + q, tm // 16, stride=4), :], r4 + 4 * q, tm // 16)

    def write_gates(c0, unit):
        for c in range(GATE_W // HEAD_DIM):
            z = acc_scr[c0 + c]
            gate_ref[unit, :, c * HEAD_DIM:(c + 1) * HEAD_DIM] = (0.5 + 0.5 * jnp.tanh(0.5 * z)).astype(BF16)

    def write_gate_tile():
        write_gates(0, 0)
        write_gates(GATE_W // HEAD_DIM, 1)

    def write_gate_tail():
        write_gates(0, 0)
        gate_ref[1] = jnp.zeros(gate_ref.shape[1:], BF16)

    @pl.when(has_matmul & (p == 0))
    def _():
        @pl.when(s > 0)
        def _():
            write_gate_tile()
        rows = 256
        for c in range(tm // rows):
            x = x_ref[c * rows:(c + 1) * rows, :]
            h_scr[c * rows:(c + 1) * rows, :] = _rms_scale(x, g1_ref[...]).astype(BF16)
        matmul_tail()

    @pl.when(s == n_steps - 1)
    def _():
        write_gate_tile()

    @pl.when(has_matmul & (p == 1))
    def _():
        write_gate_tail()
        matmul_tile()

    mid_row = has_matmul & (p > 1)
    role = t // N_GROUPS
    group = t % N_GROUPS
    is_a = mid_row & (t < A_TILES)

    for g, d in enumerate(DILATIONS):
        @pl.when(is_a & (group == g) & (role < 2))
        def _(d=d):
            gain = jnp.where(role == 0, qkg_ref[0:1, :], qkg_ref[1:2, :])
            for c in range(HEADS_PER_TILE):
                write_head(c, d, gain)
            matmul_tile()

        @pl.when(is_a & (group == g) & (role == 2))
        def _(d=d):
            for c in range(HEADS_PER_TILE):
                write_head(c, d, None)
            matmul_tile()

    @pl.when(mid_row & (t >= QB_TILE0) & (t < KVB_TILE))
    def _():
        for c in range(HEADS_PER_TILE):
            write_head(c, 1, qkg_ref[2:3, :])
        matmul_tile()

    @pl.when(mid_row & (t == KVB_TILE))
    def _():
        for c in range(B_KV_HEADS):
            write_head(c, 1, qkg_ref[3:4, :])
        for c in range(B_KV_HEADS, HEADS_PER_TILE):
            write_head(c, 1, None)
        write_gate_tile()
        matmul_tile()

    @pl.when(mid_row & (t > KVB_TILE))
    def _():
        write_gate_tile()
        matmul_tile()


def _in_projection(x2, g1, w, w_tail, qk_gains):
    m = x2.shape[0]
    tm = TM_IN
    n_rows = m // tm
    n_steps = n_rows * N_TILES_IN + 1

    def row_of(s):
        return jnp.maximum(s - 1, 0) // N_TILES_IN

    def tile_of(s):
        return jnp.maximum(s - 1, 0) % N_TILES_IN - 1

    out_shapes = [
        jax.ShapeDtypeStruct((N_HEADS_OUT, n_rows, tm, HEAD_DIM), BF16),
        jax.ShapeDtypeStruct((GATE_UNITS, m, GATE_W), BF16),
    ]
    out_specs = [
        pl.BlockSpec((HEADS_PER_TILE, None, tm, HEAD_DIM),
                     lambda s: (jnp.clip(tile_of(s), 0, KVB_TILE), row_of(s), 0, 0)),
        pl.BlockSpec((2, tm, GATE_W),
                     lambda s: (jnp.where(tile_of(s) >= KVB_TILE, tile_of(s) - KVB_TILE, TAIL_TILE - KVB_TILE),
                                row_of(s), 0)),
    ]
    in_specs = [
        pl.BlockSpec((tm, D_MODEL), lambda s: (jnp.minimum(s // N_TILES_IN, n_rows - 1), 0)),
        pl.BlockSpec((1, D_MODEL), lambda s: (0, 0)),
        pl.BlockSpec((D_MODEL, TN_IN),
                     lambda s: (0, jnp.where(s < n_steps - 1, jnp.maximum(s % N_TILES_IN - 1, 0),
                                             TAIL_TILE - 1))),
        pl.BlockSpec((D_MODEL, TAIL_COLS), lambda s: (0, 0), pipeline_mode=pl.Buffered(1)),
        pl.BlockSpec((4, HEAD_DIM), lambda s: (0, 0)),
    ]
    return pl.pallas_call(
        functools.partial(_inproj_kernel, n_steps=n_steps),
        grid=(n_steps,),
        in_specs=in_specs,
        out_specs=out_specs,
        out_shape=out_shapes,
        scratch_shapes=[pltpu.VMEM((tm, D_MODEL), BF16),
                        pltpu.VMEM((HEADS_PER_TILE, tm, HEAD_DIM), F32),
                        pltpu.VMEM((2, tm, HEAD_DIM), F32)],
        compiler_params=pltpu.CompilerParams(
            dimension_semantics=("arbitrary",), vmem_limit_bytes=VMEM_LIMIT),
        name="in_projection",
    )(x2, g1, w, w_tail, qk_gains)


def _band_bias(slope_times_stride, max_back):
    qi = lax.broadcasted_iota(jnp.int32, (BLOCK, 2 * BLOCK), 0) + BLOCK
    ki = lax.broadcasted_iota(jnp.int32, (BLOCK, 2 * BLOCK), 1)
    rel = qi - ki
    valid = (rel >= 0) & (rel <= max_back)
    bias = jnp.where(valid, -slope_times_stride * rel.astype(F32), NEG_INF)
    return bias, ki


def _with_ones(vw):
    return jnp.concatenate([vw, jnp.ones(vw.shape, vw.dtype)], axis=-1)


def _attn_units(units):
    scores = [_dot_nt(q, kw) + bias for q, kw, _, bias in units]
    maxes = [jnp.max(s, axis=-1, keepdims=True) for s in scores]
    probs = [jnp.exp2(s - m).astype(BF16) for s, m in zip(scores, maxes)]
    outs = [_dot(p, _with_ones(u[2])) for p, u in zip(probs, units)]
    results = []
    for o_l, m in zip(outs, maxes):
        o, l = o_l[:, :HEAD_DIM], o_l[:, HEAD_DIM:]
        results.append((o / l, m + jnp.log2(l)))
    return results


UNITS_PER_BATCH = 16
TILES_PER_SUPER = SUPER // TM_IN


A_HEADS_PER_STEP = 2


def _mixer_a_kernel(slopes_ref, *refs):
    qkv_refs, (w1_ref, w2_ref) = refs[:9], refs[9:11]
    o_ref, w1_out_ref, w2_out_ref = refs[11:14]
    work_scrs, prev_refs = refs[14:20], refs[20:26]
    w1_out_ref[...] = w1_ref[...].astype(BF16)
    w2_out_ref[...] = w2_ref[...].astype(BF16)

    first = pl.program_id(2) == 0

    @pl.when(first)
    def _():
        for ref in prev_refs:
            ref[...] = jnp.zeros(ref.shape, ref.dtype)

    for hh in range(A_HEADS_PER_STEP):
        _mixer_a_head(slopes_ref, pl.program_id(1) * A_HEADS_PER_STEP + hh, first,
                      *[ref.at[hh] for ref in qkv_refs], o_ref.at[hh], *work_scrs,
                      *[ref.at[hh] for ref in prev_refs])


def _mixer_a_head(slopes_ref, h, first, q0_ref, k0c_ref, v0c_ref, q1_ref, k1c_ref, v1c_ref, q2_ref, k2c_ref, v2c_ref,
                  o_ref, o1_scr, l1_scr, o2_scr, l2_scr, o4_scr, l4_scr,
                  k0p_ref, v0p_ref, k1p_ref, v1p_ref, k2p_ref, v2p_ref):
    biases = []
    for g, d in enumerate(DILATIONS):
        bias, ki = _band_bias(slopes_ref[g * A_HPG + h] * float(d), A_MAX_BACK)
        bias_head = jnp.where(first, jnp.where(ki < BLOCK, NEG_INF, bias), bias)
        biases.append((bias, bias_head))

    def cat(*parts):
        return jnp.concatenate(parts, axis=0)

    def block16(ref, r):
        return cat(ref[0, r], ref[1, r])

    def body2(it, carry):
        rs = [it * UNITS_PER_BATCH + u for u in range(UNITS_PER_BATCH)]
        units = [(block16(q2_ref, r), cat(block16(k2p_ref, r), block16(k2c_ref, r)),
                  cat(block16(v2p_ref, r), block16(v2c_ref, r)), biases[2][1]) for r in rs]
        for r, (o, lse) in zip(rs, _attn_units(units)):
            rows = pl.ds((r % 4) * (SUPER // 4) + r // 4, BLOCK, stride=4)
            o4_scr[rows, :] = o
            l4_scr[rows, :] = lse
        return carry

    lax.fori_loop(0, 16 // UNITS_PER_BATCH, body2, 0)
    for r4 in range(4):
        seg = slice(r4 * (SUPER // 4), (r4 + 1) * (SUPER // 4))
        o2_scr[pl.ds(r4, SUPER // 4, stride=4), :] = o4_scr[seg, :]
        l2_scr[pl.ds(r4, SUPER // 4, stride=4), :] = l4_scr[seg, :]

    def block4(ref, r, blk):
        half = (blk % 2) * BLOCK
        return ref[blk // 2, r, half:half + BLOCK]

    def window4(cur_ref, prev_ref, r, blk):
        prev = prev_ref[r] if blk == 0 else block4(cur_ref, r, blk - 1)
        return cat(prev, block4(cur_ref, r, blk))

    def body1(it, carry):
        per = UNITS_PER_BATCH // 4
        rb = [(it * per + u // 4, u % 4) for u in range(UNITS_PER_BATCH)]
        units = [(block4(q1_ref, r, blk), window4(k1c_ref, k1p_ref, r, blk), window4(v1c_ref, v1p_ref, r, blk),
                  biases[1][1] if blk == 0 else biases[1][0]) for r, blk in rb]
        for (r, blk), (o, lse) in zip(rb, _attn_units(units)):
            rows = pl.ds(blk * BLOCK * 4 + r, BLOCK, stride=4)
            o1_scr[rows, :] = o
            l1_scr[rows, :] = lse
        return carry

    lax.fori_loop(0, 16 // UNITS_PER_BATCH, body1, 0)

    def window1(cur_ref, prev_ref, blk):
        if blk == 0:
            return cat(prev_ref[...], cur_ref[0:BLOCK])
        return cur_ref[(blk - 1) * BLOCK:(blk + 1) * BLOCK]

    for b0 in range(0, SUPER // BLOCK, 8):
        blks = list(range(b0, b0 + 8))
        units = [(q0_ref[blk * BLOCK:(blk + 1) * BLOCK, :], window1(k0c_ref, k0p_ref, blk),
                  window1(v0c_ref, v0p_ref, blk), biases[0][1] if blk == 0 else biases[0][0])
                 for blk in blks]
        for blk, (o0, lse0) in zip(blks, _attn_units(units)):
            rows = slice(blk * BLOCK, (blk + 1) * BLOCK)
            lse1 = l1_scr[rows, :]
            lse2 = l2_scr[rows, :]
            mx = jnp.maximum(jnp.maximum(lse0, lse1), lse2)
            w0 = jnp.exp2(lse0 - mx)
            w1 = jnp.exp2(lse1 - mx)
            w2 = jnp.exp2(lse2 - mx)
            o = (w0 * o0 + w1 * o1_scr[rows, :] + w2 * o2_scr[rows, :]) / (w0 + w1 + w2)
            o_ref[rows, :] = o.astype(BF16)

    k0p_ref[...] = k0c_ref[SUPER - BLOCK:SUPER]
    v0p_ref[...] = v0c_ref[SUPER - BLOCK:SUPER]
    k1p_ref[...] = k1c_ref[TILES_PER_SUPER - 1, :, TM_IN // 4 - BLOCK:TM_IN // 4]
    v1p_ref[...] = v1c_ref[TILES_PER_SUPER - 1, :, TM_IN // 4 - BLOCK:TM_IN // 4]
    k2p_ref[...] = k2c_ref[...]
    v2p_ref[...] = v2c_ref[...]


def _mixer_a(heads, slopes_a, w_ff1, w_ff2, bsz, seq):
    m = bsz * seq
    nsb = seq // SUPER
    n_rows = m // TM_IN
    hs = A_HEADS_PER_STEP
    pairs = A_HPG // hs
    grid = (bsz, pairs, nsb)
    n_steps = bsz * nsb * pairs
    assert w_ff1.shape[0] % (16 * n_steps) == 0 and w_ff2.shape[0] % (16 * n_steps) == 0

    def weight_slice(w):
        return pl.BlockSpec((w.shape[0] // n_steps, w.shape[1]), lambda b, h, s: ((b * pairs + h) * nsb + s, 0))
    nat = heads.reshape(N_HEADS_OUT, m, HEAD_DIM)
    by4 = heads.reshape(N_HEADS_OUT, n_rows, 4, TM_IN // 4, HEAD_DIM)
    by16 = heads.reshape(N_HEADS_OUT, n_rows, 16, TM_IN // 16, HEAD_DIM)

    def head(role, g, h):
        return (role * N_GROUPS + g) * pairs + h

    def nat_cur(role):
        return pl.BlockSpec((hs, SUPER, HEAD_DIM), lambda b, h, s: (head(role, 0, h), b * nsb + s, 0))

    def dil_cur(role, g, d):
        return pl.BlockSpec((hs, TILES_PER_SUPER, d, TM_IN // d, HEAD_DIM),
                            lambda b, h, s: (head(role, g, h), b * nsb + s, 0, 0, 0))

    in_specs = [pl.BlockSpec(memory_space=pltpu.SMEM),
                nat_cur(0), nat_cur(1), nat_cur(2),
                dil_cur(0, 1, 4), dil_cur(1, 1, 4), dil_cur(2, 1, 4),
                dil_cur(0, 2, 16), dil_cur(1, 2, 16), dil_cur(2, 2, 16),
                weight_slice(w_ff1), weight_slice(w_ff2)]
    prev_shapes = [(hs, BLOCK, HEAD_DIM), (hs, 4, BLOCK, HEAD_DIM), (hs, TILES_PER_SUPER, 16, TM_IN // 16, HEAD_DIM)]
    return pl.pallas_call(
        _mixer_a_kernel,
        grid=grid,
        in_specs=in_specs,
        out_specs=[pl.BlockSpec((hs, SUPER, HEAD_DIM), lambda b, h, s: (h, b * nsb + s, 0)),
                   weight_slice(w_ff1), weight_slice(w_ff2)],
        out_shape=[jax.ShapeDtypeStruct((A_HPG, m, HEAD_DIM), BF16),
                   jax.ShapeDtypeStruct(w_ff1.shape, BF16), jax.ShapeDtypeStruct(w_ff2.shape, BF16)],
        scratch_shapes=([pltpu.VMEM((SUPER, HEAD_DIM), F32) for _ in range(6)]
                        + [pltpu.VMEM(shape, BF16) for shape in prev_shapes for _ in range(2)]),
        compiler_params=pltpu.CompilerParams(
            dimension_semantics=("arbitrary", "arbitrary", "arbitrary"), vmem_limit_bytes=VMEM_LIMIT),
        name="mixer_a",
    )(slopes_a, nat, nat, nat, by4, by4, by4, by16, by16, by16, w_ff1, w_ff2)


def _mixer_b_kernel(slopes_ref, sinks_ref, q_ref, kc_ref, kp_ref, vc_ref, vp_ref, *rest, tiles_per_seq):
    n_w = (len(rest) - 1) // 2
    w_refs, o_ref, w_out_refs = rest[:n_w], rest[n_w], rest[n_w + 1:]
    for w_ref, w_out_ref in zip(w_refs, w_out_refs):
        w_out_ref[...] = w_ref[...].astype(BF16)

    first = (pl.program_id(0) % tiles_per_seq) == 0
    kvh = pl.program_id(1)
    tq = kc_ref.shape[0]

    assert B_MAX_BACK == BLOCK - 1
    row = lax.broadcasted_iota(jnp.int32, (BLOCK, BLOCK), 0)
    lane = lax.broadcasted_iota(jnp.int32, (BLOCK, BLOCK), 1)
    from_prev = lane > row
    rel = jnp.where(from_prev, BLOCK + row - lane, row - lane).astype(F32)
    biases, biases_head = [], []
    for hh in range(B_GROUP):
        bias = -slopes_ref[kvh * B_GROUP + hh] * rel
        biases.append(bias)
        biases_head.append(jnp.where(first, jnp.where(from_prev, NEG_INF, bias), bias))

    for blk in range(tq // BLOCK):
        if blk == 0:
            kw = jnp.concatenate([kp_ref[...], kc_ref[0:BLOCK]], axis=0)
            vw = jnp.concatenate([vp_ref[...], vc_ref[0:BLOCK]], axis=0)
        else:
            kw = kc_ref[(blk - 1) * BLOCK:(blk + 1) * BLOCK]
            vw = vc_ref[(blk - 1) * BLOCK:(blk + 1) * BLOCK]
        q = q_ref[:, blk * BLOCK:(blk + 1) * BLOCK, :].reshape(B_GROUP * BLOCK, HEAD_DIM)
        s_all = _dot_nt(q, kw)
        ps, sink_terms = [], []
        for hh in range(B_GROUP):
            s2 = s_all[hh * BLOCK:(hh + 1) * BLOCK]
            s = jnp.where(from_prev, s2[:, :BLOCK], s2[:, BLOCK:]) + (biases_head[hh] if blk == 0 else biases[hh])
            sink = sinks_ref[kvh * B_GROUP + hh]
            mrow = jnp.maximum(jnp.max(s, axis=-1, keepdims=True), sink)
            p = jnp.exp2(s - mrow)
            ps.append(jnp.concatenate([jnp.where(from_prev, p, 0.0).astype(BF16),
                                       jnp.where(from_prev, 0.0, p).astype(BF16)], axis=-1))
            sink_terms.append(jnp.exp2(sink - mrow))
        o_all = _dot(jnp.concatenate(ps, axis=0), _with_ones(vw))
        for hh in range(B_GROUP):
            rows = slice(hh * BLOCK, (hh + 1) * BLOCK)
            o = o_all[rows, :HEAD_DIM] / (o_all[rows, HEAD_DIM:] + sink_terms[hh])
            o_ref[hh, blk * BLOCK:(blk + 1) * BLOCK, :] = o.astype(BF16)


def _mixer_b(heads, slopes_b, sinks, weights, bsz, seq):
    m = bsz * seq
    tq = TQ_B
    per = tq // BLOCK
    grid = (m // tq, B_KV_HEADS)
    n_steps = grid[0] * grid[1]
    assert all(w.shape[0] % (16 * n_steps) == 0 for w in weights)
    nat = heads.reshape(N_HEADS_OUT, m, HEAD_DIM)

    def weight_slice(w):
        return pl.BlockSpec((w.shape[0] // n_steps, w.shape[1]), lambda i, c: (i * B_KV_HEADS + c, 0))

    def cur(head0):
        return pl.BlockSpec((None, tq, HEAD_DIM), lambda i, c: (head0 + c, i, 0))

    def prev(head0):
        return pl.BlockSpec((None, BLOCK, HEAD_DIM), lambda i, c: (head0 + c, jnp.maximum(i * per - 1, 0), 0))

    smem = pl.BlockSpec(memory_space=pltpu.SMEM)
    return pl.pallas_call(
        functools.partial(_mixer_b_kernel, tiles_per_seq=seq // tq),
        grid=grid,
        in_specs=[smem, smem,
                  pl.BlockSpec((B_GROUP, tq, HEAD_DIM), lambda i, c: (QB_HEAD0 // B_GROUP + c, i, 0)),
                  cur(KB_HEAD0), prev(KB_HEAD0), cur(VB_HEAD0), prev(VB_HEAD0)]
                 + [weight_slice(w) for w in weights],
        out_specs=[pl.BlockSpec((B_GROUP, tq, HEAD_DIM), lambda i, c: (c, i, 0))]
                  + [weight_slice(w) for w in weights],
        out_shape=[jax.ShapeDtypeStruct((B_Q_HEADS, m, HEAD_DIM), BF16)]
                  + [jax.ShapeDtypeStruct(w.shape, BF16) for w in weights],
        compiler_params=pltpu.CompilerParams(
            dimension_semantics=("arbitrary", "arbitrary"), vmem_limit_bytes=VMEM_LIMIT),
        name="mixer_b",
    )(slopes_b, sinks, nat, nat, nat, nat, nat, *weights)


def _merge_kernel(oa_ref, ob_ref, gate_ref, wa_ref, wb_ref, out_ref):
    oa = jnp.concatenate([oa_ref[hh] for hh in range(A_HPG)], axis=-1)
    ob = jnp.concatenate([ob_ref[hh] for hh in range(B_Q_HEADS)], axis=-1)
    per = D_MODEL // GATE_W
    ga = jnp.concatenate([gate_ref[GATE_UNIT0 + c] for c in range(per)], axis=-1).astype(F32)
    gb = jnp.concatenate([gate_ref[GATE_UNIT0 + per + c] for c in range(per)], axis=-1).astype(F32)
    merged = ga * _dot(oa, wa_ref[...]) + gb * _dot(ob, wb_ref[...])
    out_ref[...] = merged.astype(BF16)


def _merge(oa, ob, gates, wa, wb):
    m = gates.shape[1]
    tm = TM_MERGE
    resident = dict(pipeline_mode=pl.Buffered(1))
    return pl.pallas_call(
        _merge_kernel,
        grid=(m // tm,),
        in_specs=[pl.BlockSpec((A_HPG, tm, HEAD_DIM), lambda i: (0, i, 0)),
                  pl.BlockSpec((B_Q_HEADS, tm, HEAD_DIM), lambda i: (0, i, 0)),
                  pl.BlockSpec((GATE_UNITS, tm, GATE_W), lambda i: (0, i, 0)),
                  pl.BlockSpec(wa.shape, lambda i: (0, 0), **resident),
                  pl.BlockSpec(wb.shape, lambda i: (0, 0), **resident)],
        out_specs=pl.BlockSpec((tm, D_MODEL), lambda i: (i, 0)),
        out_shape=jax.ShapeDtypeStruct((m, D_MODEL), BF16),
        compiler_params=pltpu.CompilerParams(
            dimension_semantics=("arbitrary",), vmem_limit_bytes=VMEM_LIMIT),
        name="gated_merge",
    )(oa, ob, gates, wa, wb)


def _ffn_kernel(x_ref, mg_ref, wo_ref, g2_ref, w1_ref, w2_ref, out_ref, h2_scr):
    @pl.when(pl.program_id(1) == 0)
    def _():
        x1 = x_ref[...] + _dot(mg_ref[...], wo_ref[...])
        out_ref[...] = x1
        h2_scr[...] = _rms_scale(x1, g2_ref[...]).astype(BF16)

    z = _dot(h2_scr[...], w1_ref[...])
    act = jnp.square(jnp.maximum(z, 0.0)).astype(BF16)
    out_ref[...] += _dot(act, w2_ref[...])


def _ffn(x2, merged, wo, g2, w1, w2):
    m = x2.shape[0]
    tm, tf = TM_FFN, TF_FFN
    return pl.pallas_call(
        _ffn_kernel,
        grid=(m // tm, D_FF // tf),
        in_specs=[pl.BlockSpec((tm, D_MODEL), lambda i, f: (i, 0), pipeline_mode=pl.Buffered(1)),
                  pl.BlockSpec((tm, D_MODEL), lambda i, f: (i, 0), pipeline_mode=pl.Buffered(1)),
                  pl.BlockSpec(wo.shape, lambda i, f: (0, 0), pipeline_mode=pl.Buffered(1)),
                  pl.BlockSpec((1, D_MODEL), lambda i, f: (0, 0)),
                  pl.BlockSpec((D_MODEL, tf), lambda i, f: (0, f)),
                  pl.BlockSpec((tf, D_MODEL), lambda i, f: (f, 0))],
        out_specs=pl.BlockSpec((tm, D_MODEL), lambda i, f: (i, 0)),
        out_shape=jax.ShapeDtypeStruct((m, D_MODEL), F32),
        scratch_shapes=[pltpu.VMEM((tm, D_MODEL), BF16)],
        compiler_params=pltpu.CompilerParams(
            dimension_semantics=("arbitrary", "arbitrary"), vmem_limit_bytes=VMEM_LIMIT),
        name="out_proj_ffn",
    )(x2, merged, wo, g2, w1, w2)


def kernel(x, norm1_g, w_in, q_norm_a, k_norm_a, q_norm_b, k_norm_b, sinks_b, w_branch_a, w_branch_b,
           w_out, norm2_g, w_ff1, w_ff2):
    bsz, seq, d_model = x.shape
    depth = w_in.shape[0]
    assert d_model == D_MODEL and w_in.shape[2] == IN_COLS
    assert seq % SUPER == 0 and seq % TQ_B == 0 and SUPER % TM_IN == 0
    m = bsz * seq
    slopes = _alibi_slopes() * np.float32(LOG2E)
    slopes_b = jnp.asarray(slopes[:B_Q_HEADS])
    slopes_a = jnp.asarray(slopes[B_Q_HEADS:])
    scale = HEAD_DIM ** -0.5 * LOG2E

    x2 = x.reshape(m, D_MODEL)
    for l in range(depth):
        qk_gains = jnp.stack([q_norm_a[l] * scale, k_norm_a[l], q_norm_b[l] * scale, k_norm_b[l]])
        w_tail = w_in[l][:, TAIL_TILE * TN_IN:].astype(BF16)
        heads, gates = _in_projection(x2, norm1_g[l].reshape(1, D_MODEL), w_in[l].astype(BF16), w_tail, qk_gains)
        oa, w1_bf16, w2_bf16 = _mixer_a(heads, slopes_a, w_ff1[l], w_ff2[l], bsz, seq)
        ob, wa_bf16, wb_bf16, wo_bf16 = _mixer_b(heads, slopes_b, sinks_b[l] * LOG2E,
                                                 (w_branch_a[l], w_branch_b[l], w_out[l]), bsz, seq)
        merged = _merge(oa, ob, gates, wa_bf16, wb_bf16)
        x2 = _ffn(x2, merged, wo_bf16, norm2_g[l].reshape(1, D_MODEL),
                  w1_bf16, w2_bf16)
    return x2.reshape(bsz, seq, D_MODEL)
```
